```python
import jax, jax.numpy as jnp
from jax import lax
import numpy as np

D_MODEL = 1024
BATCH = 4
SEQ = 4096
DEPTH = 2

MLA_HEADS = 8
QK_NOPE = 64
QK_ROPE = 32
V_HEAD = 64
Q_RANK = 256
KV_RANK = 128
ROPE_THETA = 10000.0
Q_BLOCK = 128
MLA_OUT = MLA_HEADS * V_HEAD
FNET_GROUPS = 8
FNET_GROUP_DIM = 64
FNET_WIDTH = FNET_GROUPS * FNET_GROUP_DIM
PROJ_A = Q_RANK + KV_RANK + QK_ROPE + FNET_WIDTH
D_MIX_A = MLA_OUT + FNET_WIDTH
CONV_WIDTH = 3
D_CONV = D_MODEL
N_EXPERTS = 32
TOP_K = 4
D_EXPERT = D_MODEL
SWIGLU_LIMIT = 7.0
SWIGLU_ALPHA = 1.702
MOE_BLOCK = 128
DEEPNORM_ALPHA = (2 * DEPTH) ** 0.25
DEEPNORM_BETA = (8 * DEPTH) ** -0.25
N_EVEN = (DEPTH + 1) // 2
N_ODD = DEPTH // 2
LN_EPS = 1e-5
MOD_EPS = 1e-6
RMS_EPS = 1e-6

kernel_name = "hybrid_mla_fnet_shortconv_moe_deepnorm_adaln"


def layer_norm(x, g=None, b=None, eps=LN_EPS):
    xf = x.astype(jnp.float32)
    mu = jnp.mean(xf, axis=-1, keepdims=True)
    var = jnp.mean(jnp.square(xf - mu), axis=-1, keepdims=True)
    y = (xf - mu) * lax.rsqrt(var + eps)
    if g is not None:
        y = y * g.astype(jnp.float32) + b.astype(jnp.float32)
    return y.astype(x.dtype)


def rms_norm(x, g, eps=RMS_EPS):
    xf = x.astype(jnp.float32)
    y = xf * lax.rsqrt(jnp.mean(jnp.square(xf), axis=-1, keepdims=True) + eps)
    return (y * g.astype(jnp.float32)).astype(x.dtype)


def rope_tables(positions, dtype):
    inv_freq = ROPE_THETA ** (-jnp.arange(0, QK_ROPE, 2, dtype=jnp.float32) / QK_ROPE)
    ang = positions.astype(jnp.float32)[..., None] * inv_freq
    return jnp.cos(ang).astype(dtype), jnp.sin(ang).astype(dtype)


def apply_rope(t, cos, sin):
    t1, t2 = jnp.split(t, 2, axis=-1)
    return jnp.concatenate([t1 * cos - t2 * sin, t2 * cos + t1 * sin], axis=-1)


def mla_attention(c_q, c_kv, k_r, cos, sin, q_norm_g, kv_norm_g, w_uq, w_uk, w_uv):
    B, S = c_q.shape[:2]
    q = (rms_norm(c_q, q_norm_g) @ w_uq).reshape(B, S, MLA_HEADS, QK_NOPE + QK_ROPE)
    scale = (QK_NOPE + QK_ROPE) ** -0.5
    q_nope = q[..., :QK_NOPE] * scale
    q_pe = apply_rope(q[..., QK_NOPE:], cos[:, :, None, :], sin[:, :, None, :]) * scale
    ckv = rms_norm(c_kv, kv_norm_g)
    k_nope = (ckv @ w_uk).reshape(B, S, MLA_HEADS, QK_NOPE)
    v = (ckv @ w_uv).reshape(B, S, MLA_HEADS, V_HEAD)
    k_pe = apply_rope(k_r, cos, sin)
    nb = S // Q_BLOCK

    def to_blocks(t):
        return jnp.swapaxes(t.reshape((B, nb, Q_BLOCK) + t.shape[2:]), 0, 1)

    def attend(blk):
        qn, qp = blk
        s = (jnp.einsum('bqhd,bkhd->bhqk', qn, k_nope)
             + jnp.einsum('bqhr,bkr->bhqk', qp, k_pe))
        p = jax.nn.softmax(s.astype(jnp.float32), axis=-1).astype(v.dtype)
        return jnp.einsum('bhqk,bkhd->bqhd', p, v)

    o = lax.map(attend, (to_blocks(q_nope), to_blocks(q_pe)))
    return jnp.swapaxes(o, 0, 1).reshape(B, S, MLA_OUT)


def fnet_mix(u):
    B, S = u.shape[:2]
    ug = u.reshape(B, S, FNET_GROUPS, FNET_GROUP_DIM).astype(jnp.float32)
    f = jnp.real(jnp.fft.fft2(ug, axes=(1, 3), norm='ortho'))
    return f.reshape(B, S, FNET_WIDTH).astype(u.dtype)


def short_conv(u, w):
    S = u.shape[1]
    half = CONV_WIDTH // 2
    up = jnp.pad(u, ((0, 0), (half, CONV_WIDTH - 1 - half), (0, 0)))
    return sum(up[:, j:j + S] * w[j] for j in range(CONV_WIDTH))


def clamped_swiglu(g, l):
    g = jnp.minimum(g, SWIGLU_LIMIT)
    l = jnp.clip(l, -SWIGLU_LIMIT, SWIGLU_LIMIT)
    return g * jax.nn.sigmoid(SWIGLU_ALPHA * g) * (l + 1.0)


def moe(h, w_router, b_router, w_glu, b_glu, w_lin, b_lin, w_down, b_down):
    B, S, D = h.shape
    T = B * S
    hf = h.reshape(T, D)
    logits = hf.astype(jnp.float32) @ w_router.astype(jnp.float32) + b_router.astype(jnp.float32)
    top_v, top_i = lax.top_k(logits, TOP_K)
    gates = jax.nn.softmax(top_v, axis=-1)
    N = T * TOP_K
    flat_e = top_i.reshape(N).astype(jnp.int32)
    flat_tok = jnp.repeat(jnp.arange(T, dtype=jnp.int32), TOP_K)
    flat_w = gates.reshape(N)
    order = jnp.argsort(flat_e, stable=True)
    se, stok, sw = flat_e[order], flat_tok[order], flat_w[order]
    counts = jnp.bincount(flat_e, length=N_EXPERTS).astype(jnp.int32)
    start = jnp.cumsum(counts) - counts
    padded = (counts + MOE_BLOCK - 1) // MOE_BLOCK * MOE_BLOCK
    pend = jnp.cumsum(padded)
    pstart = pend - padded
    dest = pstart[se] + jnp.arange(N, dtype=jnp.int32) - start[se]
    NP = (N + MOE_BLOCK - 1) // MOE_BLOCK * MOE_BLOCK + N_EXPERTS * MOE_BLOCK
    NB = NP // MOE_BLOCK
    buf_tok = jnp.zeros((NP,), jnp.int32).at[dest].set(stok)
    buf_w = jnp.zeros((NP,), jnp.float32).at[dest].set(sw)
    block_e = jnp.clip(jnp.searchsorted(pend, jnp.arange(NB, dtype=jnp.int32) * MOE_BLOCK, side='right'),
                       0, N_EXPERTS - 1)

    def expert_block(args):
        tok, wt, e = args
        xb = hf[tok]
        a = clamped_swiglu(xb @ w_glu[e] + b_glu[e], xb @ w_lin[e] + b_lin[e])
        y = a @ w_down[e] + b_down[e]
        return y.astype(jnp.float32) * wt[:, None]

    y = lax.map(expert_block, (buf_tok.reshape(NB, MOE_BLOCK), buf_w.reshape(NB, MOE_BLOCK), block_e))
    out = jnp.zeros((T, D), jnp.float32).at[buf_tok].add(y.reshape(NP, D))
    return out.astype(h.dtype).reshape(B, S, D)


def setup_inputs(seed: int = 0) -> dict:
    key = jax.random.key(seed)
    ks = iter(jax.random.split(key, 40))
    f32 = jnp.float32

    def nrm(shape, scale):
        return jax.random.normal(next(ks), shape, f32) * scale

    return {
        "x": nrm((BATCH, SEQ, D_MODEL), 1.0),
        "c": nrm((BATCH, D_MODEL), 1.0),
        "positions": jnp.broadcast_to(jnp.arange(SEQ, dtype=jnp.int32), (BATCH, SEQ)),
        "ada_w": nrm((DEPTH, D_MODEL, 6 * D_MODEL), D_MODEL ** -0.5),
        "ada_b": nrm((DEPTH, 6 * D_MODEL), 0.02),
        "ln_mix_g": 1.0 + nrm((DEPTH, D_MODEL), 0.02),
        "ln_mix_b": nrm((DEPTH, D_MODEL), 0.02),
        "ln_ffn_g": 1.0 + nrm((DEPTH, D_MODEL), 0.02),
        "ln_ffn_b": nrm((DEPTH, D_MODEL), 0.02),
        "mla_w_in": nrm((N_EVEN, D_MODEL, PROJ_A), D_MODEL ** -0.5),
        "mla_q_norm_g": 1.0 + nrm((N_EVEN, Q_RANK), 0.02),
        "mla_kv_norm_g": 1.0 + nrm((N_EVEN, KV_RANK), 0.02),
        "mla_w_uq": nrm((N_EVEN, Q_RANK, MLA_HEADS * (QK_NOPE + QK_ROPE)), Q_RANK ** -0.5),
        "mla_w_uk": nrm((N_EVEN, KV_RANK, MLA_HEADS * QK_NOPE), KV_RANK ** -0.5),
        "mla_w_uv": nrm((N_EVEN, KV_RANK, MLA_HEADS * V_HEAD), KV_RANK ** -0.5),
        "mix_a_w_out": nrm((N_EVEN, D_MIX_A, D_MODEL), DEEPNORM_BETA * D_MIX_A ** -0.5),
        "conv_w_in": nrm((N_ODD, D_MODEL, 3 * D_CONV), D_MODEL ** -0.5),
        "conv_w": nrm((N_ODD, CONV_WIDTH, D_CONV), CONV_WIDTH ** -0.5),
        "conv_w_out": nrm((N_ODD, D_CONV, D_MODEL), DEEPNORM_BETA * D_CONV ** -0.5),
        "moe_w_router": nrm((DEPTH, D_MODEL, N_EXPERTS), D_MODEL ** -0.5),
        "moe_b_router": nrm((DEPTH, N_EXPERTS), 0.01),
        "moe_w_glu": nrm((DEPTH, N_EXPERTS, D_MODEL, D_EXPERT), D_MODEL ** -0.5),
        "moe_b_glu": nrm((DEPTH, N_EXPERTS, D_EXPERT), 0.01),
        "moe_w_lin": nrm((DEPTH, N_EXPERTS, D_MODEL, D_EXPERT), D_MODEL ** -0.5),
        "moe_b_lin": nrm((DEPTH, N_EXPERTS, D_EXPERT), 0.01),
        "moe_w_down": nrm((DEPTH, N_EXPERTS, D_EXPERT, D_MODEL), DEEPNORM_BETA * D_EXPERT ** -0.5),
        "moe_b_down": nrm((DEPTH, N_EXPERTS, D_MODEL), 0.01),
    }


def reference(x, c, positions, ada_w, ada_b, ln_mix_g, ln_mix_b, ln_ffn_g, ln_ffn_b,
              mla_w_in, mla_q_norm_g, mla_kv_norm_g, mla_w_uq, mla_w_uk, mla_w_uv, mix_a_w_out,
              conv_w_in, conv_w, conv_w_out,
              moe_w_router, moe_b_router, moe_w_glu, moe_b_glu, moe_w_lin, moe_b_lin,
              moe_w_down, moe_b_down):
    cond = jax.nn.silu(c)
    cos, sin = rope_tables(positions, x.dtype)
    split_a = [Q_RANK, Q_RANK + KV_RANK, Q_RANK + KV_RANK + QK_ROPE]
    for l in range(DEPTH):
        mod = cond @ ada_w[l] + ada_b[l]
        sh_m, sc_m, g_m, sh_f, sc_f, g_f = [m[:, None, :] for m in jnp.split(mod, 6, axis=-1)]
        h = layer_norm(x, eps=MOD_EPS) * (1.0 + sc_m) + sh_m
        i = l // 2
        if l % 2 == 0:
            p = h @ mla_w_in[i]
            c_q, c_kv, k_r, u_f = jnp.split(p, split_a, axis=-1)
            o_a = mla_attention(c_q, c_kv, k_r, cos, sin, mla_q_norm_g[i], mla_kv_norm_g[i],
                                mla_w_uq[i], mla_w_uk[i], mla_w_uv[i])
            o_b = fnet_mix(u_f)
            o = jnp.concatenate([o_a, o_b], axis=-1) @ mix_a_w_out[i]
        else:
            p = h @ conv_w_in[i]
            gate_b, gate_c, xv = jnp.split(p, 3, axis=-1)
            o = (gate_b * short_conv(gate_c * xv, conv_w[i])) @ conv_w_out[i]
        x = layer_norm(DEEPNORM_ALPHA * x + g_m * o, ln_mix_g[l], ln_mix_b[l])
        h = layer_norm(x, eps=MOD_EPS) * (1.0 + sc_f) + sh_f
        o = moe(h, moe_w_router[l], moe_b_router[l], moe_w_glu[l], moe_b_glu[l],
                moe_w_lin[l], moe_b_lin[l], moe_w_down[l], moe_b_down[l])
        x = layer_norm(DEEPNORM_ALPHA * x + g_f * o, ln_ffn_g[l], ln_ffn_b[l])
    return x
```

```python
import functools
import math

import numpy as np
import jax
import jax.numpy as jnp
from jax import lax
from jax.experimental import pallas as pl
from jax.experimental.pallas import tpu as pltpu

D_MODEL = 1024
BATCH = 4
SEQ = 4096
DEPTH = 2
TOKENS = BATCH * SEQ

MLA_HEADS = 8
QK_NOPE = 64
QK_ROPE = 32
V_HEAD = 64
Q_RANK = 256
KV_RANK = 128
ROPE_THETA = 10000.0
MLA_OUT = MLA_HEADS * V_HEAD
FNET_GROUPS = 8
FNET_GROUP_DIM = 64
FNET_WIDTH = FNET_GROUPS * FNET_GROUP_DIM
CONV_WIDTH = 3
N_EXPERTS = 32
TOP_K = 4
SWIGLU_LIMIT = 7.0
SWIGLU_ALPHA = 1.702
DEEPNORM_ALPHA = (2 * DEPTH) ** 0.25
LN_EPS = 1e-5
MOD_EPS = 1e-6
RMS_EPS = 1e-6
QK_SCALE = (QK_NOPE + QK_ROPE) ** -0.5

LANES = 128
HEAD_PAD = 128
VMEM_LIMIT = 56 * 1024 * 1024

ROW_TILE = 256
ATT_Q_TILE = 256
FNET_M_TILE = 1024
FNET_K_TILE = 512
RANK_TILE = 512
MOE_BLOCK = 256
MOE_ROWS = TOKENS * TOP_K + N_EXPERTS * MOE_BLOCK
MOE_NBLOCKS = MOE_ROWS // MOE_BLOCK
COMBINE_TILE = 128
NEG_BIG = -1e30

BF16 = jnp.bfloat16
F32 = jnp.float32


def _params(*sem):
    return pltpu.CompilerParams(dimension_semantics=sem, vmem_limit_bytes=VMEM_LIMIT)


def _ln(v, eps):
    mu = jnp.mean(v, axis=-1, keepdims=True)
    d = v - mu
    var = jnp.mean(d * d, axis=-1, keepdims=True)
    return d * lax.rsqrt(var + eps)


def _rms(v, g):
    return v * lax.rsqrt(jnp.mean(v * v, axis=-1, keepdims=True) + RMS_EPS) * g


def _dot(a, b):
    return jnp.dot(a, b, preferred_element_type=F32)


def _mod_kernel(c_ref, w_ref, b_ref, o_ref):
    c = c_ref[...]
    cond = c * jax.nn.sigmoid(c)
    o_ref[...] = _dot(cond, w_ref[...]) + b_ref[...]


def _modulation(c, ada_w, ada_b):
    tn = 1536
    n = 6 * D_MODEL
    c8 = jnp.zeros((8, D_MODEL), F32).at[:BATCH].set(c)
    out = pl.pallas_call(
        _mod_kernel,
        grid=(DEPTH, n // tn),
        in_specs=[
            pl.BlockSpec((8, D_MODEL), lambda l, j: (0, 0)),
            pl.BlockSpec((None, D_MODEL, tn), lambda l, j: (l, 0, j)),
            pl.BlockSpec((None, 1, tn), lambda l, j: (l, 0, j)),
        ],
        out_specs=pl.BlockSpec((None, 8, tn), lambda l, j: (l, 0, j)),
        out_shape=jax.ShapeDtypeStruct((DEPTH, 8, n), F32),
        compiler_params=_params("arbitrary", "arbitrary"),
        name="adaln_mod",
    )(c8, ada_w, ada_b.reshape(DEPTH, 1, n))
    return out[:, :BATCH].reshape(DEPTH, BATCH, 6, 1, D_MODEL)


def _mod_spec(chunk, tile):
    return pl.BlockSpec((None, None, 1, D_MODEL),
                        lambda i: ((i * tile) // SEQ, chunk, 0, 0))


def _rope(t, tab):
    c, s1, s2 = tab[:, :LANES], tab[:, LANES:2 * LANES], tab[:, 2 * LANES:]
    outs = []
    for h in range(MLA_HEADS):
        th = t[:, h * HEAD_PAD:(h + 1) * HEAD_PAD]
        outs.append(th * c + pltpu.roll(th, HEAD_PAD - QK_ROPE // 2, 1) * s1
                    + pltpu.roll(th, QK_ROPE // 2, 1) * s2)
    return jnp.concatenate(outs, axis=1)


def _pre0_kernel(x_ref, sc_ref, sh_ref, rope_ref, win_ref, gq_ref, gkv_ref, wuq_ref, wk_ref,
                 wuv_ref, dft_ref, q_ref, k_ref, v_ref, a_ref, b_ref):
    h = _ln(x_ref[...], MOD_EPS) * (1.0 + sc_ref[...]) + sh_ref[...]
    p = _dot(h.astype(BF16), win_ref[...])
    c_q = p[:, :Q_RANK]
    c_kv = p[:, Q_RANK:Q_RANK + KV_RANK]
    u_f = p[:, Q_RANK + KV_RANK:Q_RANK + KV_RANK + FNET_WIDTH]
    k_r = p[:, Q_RANK + KV_RANK + FNET_WIDTH:]
    tab = rope_ref[...]
    q = _dot(_rms(c_q, gq_ref[...]).astype(BF16), wuq_ref[...]) * QK_SCALE
    q_ref[...] = _rope(q, tab).astype(BF16)
    ckv = _rms(c_kv, gkv_ref[...]).astype(BF16)
    kin = jnp.concatenate([ckv, k_r.astype(BF16)], axis=1)
    k_ref[...] = _rope(_dot(kin, wk_ref[...]), tab).astype(BF16)
    v_ref[...] = _dot(ckv, wuv_ref[...]).astype(BF16)
    ab = _dot(u_f.astype(BF16), dft_ref[...])
    a_ref[...] = ab[:, :FNET_WIDTH].astype(BF16)
    b_ref[...] = ab[:, FNET_WIDTH:].astype(BF16)


def _channel_dft():
    n = np.arange(FNET_GROUP_DIM)
    ang = 2.0 * np.pi * ((n[:, None] * n[None, :]) % FNET_GROUP_DIM) / FNET_GROUP_DIM
    norm = 1.0 / math.sqrt(SEQ * FNET_GROUP_DIM)
    eye = np.eye(FNET_GROUPS)
    cc = np.kron(eye, np.cos(ang) * norm)
    ss = np.kron(eye, np.sin(ang) * norm)
    return jnp.asarray(np.concatenate([cc, ss], axis=1), BF16)


def _pre0(x, modl, rope_tab, w_in, gq, gkv, w_uq, w_uk, w_uv):
    tm = ROW_TILE
    nq = Q_RANK + KV_RANK
    w_in_r = jnp.concatenate(
        [w_in[:, :nq], w_in[:, nq + QK_ROPE:], w_in[:, nq:nq + QK_ROPE],
         jnp.zeros((D_MODEL, LANES - QK_ROPE), F32)], axis=1).astype(BF16)
    wuq_p = jnp.pad(w_uq.reshape(Q_RANK, MLA_HEADS, QK_NOPE + QK_ROPE),
                    ((0, 0), (0, 0), (0, HEAD_PAD - QK_NOPE - QK_ROPE)))
    wuq_p = wuq_p.reshape(Q_RANK, MLA_HEADS * HEAD_PAD).astype(BF16)
    wuk_p = jnp.pad(w_uk.reshape(KV_RANK, MLA_HEADS, QK_NOPE),
                    ((0, 0), (0, 0), (0, HEAD_PAD - QK_NOPE))).reshape(KV_RANK, MLA_HEADS * HEAD_PAD)
    place = np.zeros((LANES, MLA_HEADS, HEAD_PAD), np.float32)
    for j in range(QK_ROPE):
        place[j, :, QK_NOPE + j] = 1.0
    wk_p = jnp.concatenate([wuk_p, jnp.asarray(place.reshape(LANES, -1))], axis=0).astype(BF16)
    full = lambda shape: pl.BlockSpec(shape, lambda i: (0,) * len(shape))
    nst = SEQ // tm
    wide = MLA_HEADS * HEAD_PAD
    return pl.pallas_call(
        _pre0_kernel,
        grid=(TOKENS // tm,),
        in_specs=[
            pl.BlockSpec((tm, D_MODEL), lambda i: (i, 0)),
            _mod_spec(1, tm), _mod_spec(0, tm),
            pl.BlockSpec((tm, 3 * LANES), lambda i: (i, 0)),
            full((D_MODEL, D_MODEL)), full((1, Q_RANK)), full((1, KV_RANK)),
            full((Q_RANK, wide)), full((2 * LANES, wide)), full((KV_RANK, MLA_OUT)),
            full((FNET_WIDTH, 2 * FNET_WIDTH)),
        ],
        out_specs=[
            pl.BlockSpec((tm, wide), lambda i: (i, 0)),
            pl.BlockSpec((tm, wide), lambda i: (i, 0)),
            pl.BlockSpec((tm, MLA_OUT), lambda i: (i, 0)),
            pl.BlockSpec((tm, FNET_WIDTH), lambda i: (i % nst, i // nst)),
            pl.BlockSpec((tm, FNET_WIDTH), lambda i: (i % nst, i // nst)),
        ],
        out_shape=[
            jax.ShapeDtypeStruct((TOKENS, wide), BF16),
            jax.ShapeDtypeStruct((TOKENS, wide), BF16),
            jax.ShapeDtypeStruct((TOKENS, MLA_OUT), BF16),
            jax.ShapeDtypeStruct((SEQ, BATCH * FNET_WIDTH), BF16),
            jax.ShapeDtypeStruct((SEQ, BATCH * FNET_WIDTH), BF16),
        ],
        compiler_params=_params("arbitrary"),
        name="mla_fnet_front",
    )(x, modl, modl, rope_tab, w_in_r, gq.reshape(1, -1), gkv.reshape(1, -1), wuq_p, wk_p,
      w_uv.astype(BF16), _channel_dft())


def _attn_kernel(q_ref, k_ref, v_ref, o_ref):
    outs = []
    for hh in range(2):
        q = q_ref[:, hh * HEAD_PAD:(hh + 1) * HEAD_PAD]
        k = k_ref[:, hh * HEAD_PAD:(hh + 1) * HEAD_PAD]
        s = lax.dot_general(q, k, (((1,), (1,)), ((), ())), preferred_element_type=F32)
        m = jnp.max(s, axis=-1, keepdims=True)
        p = jnp.exp(s - m)
        l = jnp.sum(p, axis=-1, keepdims=True)
        outs.append(_dot(p.astype(BF16), v_ref[...]) / l)
    lane = lax.broadcasted_iota(jnp.int32, outs[0].shape, 1)
    o_ref[...] = jnp.where(lane < V_HEAD, outs[0], outs[1]).astype(BF16)


def _attention(q, k, v):
    tq = ATT_Q_TILE
    wide = MLA_HEADS * HEAD_PAD
    q = q.reshape(BATCH, SEQ, wide)
    k = k.reshape(BATCH, SEQ, wide)
    v = v.reshape(BATCH, SEQ, MLA_OUT)
    out = pl.pallas_call(
        _attn_kernel,
        grid=(BATCH, MLA_HEADS // 2, SEQ // tq),
        in_specs=[
            pl.BlockSpec((None, tq, 2 * HEAD_PAD), lambda b, h, i: (b, i, h)),
            pl.BlockSpec((None, SEQ, 2 * HEAD_PAD), lambda b, h, i: (b, 0, h)),
            pl.BlockSpec((None, SEQ, 2 * V_HEAD), lambda b, h, i: (b, 0, h)),
        ],
        out_specs=pl.BlockSpec((None, tq, 2 * V_HEAD), lambda b, h, i: (b, i, h)),
        out_shape=jax.ShapeDtypeStruct((BATCH, SEQ, MLA_OUT), BF16),
        compiler_params=_params("arbitrary", "arbitrary", "arbitrary"),
        name="mla_attention",
    )(q, k, v)
    return out.reshape(TOKENS, MLA_OUT)


def _fnet_kernel(c_ref, s_ref, a_ref, b_ref, o_ref, acc_ref):
    kk = pl.program_id(1)

    @pl.when(kk == 0)
    def _():
        acc_ref[...] = jnp.zeros_like(acc_ref)

    acc_ref[...] += _dot(c_ref[...], a_ref[...]) + _dot(s_ref[...], b_ref[...])

    @pl.when(kk == pl.num_programs(1) - 1)
    def _():
        o_ref[...] = acc_ref[...].astype(BF16)


def _sequence_dft():
    r = int(math.isqrt(SEQ))
    k = jnp.arange(SEQ, dtype=jnp.int32)[:, None]
    j = jnp.arange(r, dtype=jnp.int32)[None, :]
    alpha = (2.0 * np.pi / r) * ((k * j) % r).astype(F32)
    beta = (2.0 * np.pi / SEQ) * ((k * j) % SEQ).astype(F32)
    ca, sa, cb, sb = jnp.cos(alpha), jnp.sin(alpha), jnp.cos(beta), jnp.sin(beta)
    cs = ca[:, :, None] * cb[:, None, :] - sa[:, :, None] * sb[:, None, :]
    sn = -(sa[:, :, None] * cb[:, None, :] + ca[:, :, None] * sb[:, None, :])
    return cs.reshape(SEQ, SEQ).astype(BF16), sn.reshape(SEQ, SEQ).astype(BF16)


def _fnet(a_t, b_t):
    tm, tk = FNET_M_TILE, FNET_K_TILE
    n = BATCH * FNET_WIDTH
    cs, sn = _sequence_dft()
    return pl.pallas_call(
        _fnet_kernel,
        grid=(SEQ // tm, SEQ // tk),
        in_specs=[
            pl.BlockSpec((tm, tk), lambda i, kk: (i, kk)),
            pl.BlockSpec((tm, tk), lambda i, kk: (i, kk)),
            pl.BlockSpec((tk, n), lambda i, kk: (kk, 0)),
            pl.BlockSpec((tk, n), lambda i, kk: (kk, 0)),
        ],
        out_specs=pl.BlockSpec((tm, n), lambda i, kk: (i, 0)),
        out_shape=jax.ShapeDtypeStruct((SEQ, n), BF16),
        scratch_shapes=[pltpu.VMEM((tm, n), F32)],
        compiler_params=_params("arbitrary", "arbitrary"),
        name="fnet_seq_dft",
    )(cs, sn, a_t, b_t)


def _top4(logits):
    rows = logits.shape[0]
    lane = lax.broadcasted_iota(jnp.int32, (rows, LANES), 1).astype(F32)
    work = logits
    vals, idxs = [], []
    for _ in range(TOP_K):
        m = jnp.max(work, axis=-1, keepdims=True)
        idx = jnp.min(jnp.where(work == m, lane, float(LANES)), axis=-1, keepdims=True)
        vals.append(m)
        idxs.append(idx.astype(jnp.int32))
        work = jnp.where(lane == idx, -jnp.inf, work)
    es = [jnp.exp(v - vals[0]) for v in vals]
    den = es[0] + es[1] + es[2] + es[3]
    lane8 = lax.broadcasted_iota(jnp.int32, (rows, 8), 1)
    top_e = jnp.zeros((rows, 8), jnp.int32)
    gates = jnp.zeros((rows, 8), F32)
    for kk in range(TOP_K):
        top_e = jnp.where(lane8 == kk, idxs[kk], top_e)
        gates = jnp.where(lane8 == kk, es[kk] / den, gates)
    return top_e, gates


def _mixer_tail(o, x_ref, gm_ref, scf_ref, shf_ref, lng_ref, lnb_ref, wr_ref, br_ref,
                xo_ref, h_ref, te_ref, gt_ref):
    xn = _ln(DEEPNORM_ALPHA * x_ref[...] + gm_ref[...] * o, LN_EPS) * lng_ref[...] + lnb_ref[...]
    xo_ref[...] = xn
    h = _ln(xn, MOD_EPS) * (1.0 + scf_ref[...]) + shf_ref[...]
    h_ref[...] = h
    logits = _dot(h, wr_ref[...]) + br_ref[...]
    top_e, gates = _top4(logits)
    te_ref[...] = top_e
    gt_ref[...] = gates


def _post0_kernel(oa_ref, ob_ref, wa_ref, wb_ref, *rest):
    o = _dot(oa_ref[...], wa_ref[...]) + _dot(ob_ref[...], wb_ref[...])
    _mixer_tail(o, *rest)


def _post1_kernel(u_ref, up_ref, un_ref, gb_ref, cw_ref, wo_ref, *rest):
    tm = u_ref.shape[0]
    i = pl.program_id(0)
    s0 = (i * tm) % SEQ
    u = u_ref[...].astype(F32)
    prev_row = jnp.where(s0 > 0, up_ref[...].astype(F32)[15:16, :], 0.0)
    next_row = jnp.where(s0 + tm < SEQ, un_ref[...].astype(F32)[0:1, :], 0.0)
    row = lax.broadcasted_iota(jnp.int32, u.shape, 0)
    u_m1 = jnp.where(row == 0, prev_row, pltpu.roll(u, 1, 0))
    u_p1 = jnp.where(row == tm - 1, next_row, pltpu.roll(u, tm - 1, 0))
    cw = cw_ref[...]
    y = u_m1 * cw[0:1, :] + u * cw[1:2, :] + u_p1 * cw[2:3, :]
    g = gb_ref[...].astype(F32) * y
    o = _dot(g.astype(BF16), wo_ref[...])
    _mixer_tail(o, *rest)


def _tail_specs(tm):
    row = lambda w: pl.BlockSpec((tm, w), lambda i: (i, 0))
    full = lambda shape: pl.BlockSpec(shape, lambda i: (0,) * len(shape))
    in_specs = [row(D_MODEL), _mod_spec(2, tm), _mod_spec(4, tm), _mod_spec(3, tm),
                full((1, D_MODEL)), full((1, D_MODEL)), full((D_MODEL, LANES)), full((1, LANES))]
    out_specs = [row(D_MODEL), row(D_MODEL), row(8), row(8)]
    out_shape = [jax.ShapeDtypeStruct((TOKENS, D_MODEL), F32),
                 jax.ShapeDtypeStruct((TOKENS, D_MODEL), F32),
                 jax.ShapeDtypeStruct((TOKENS, 8), jnp.int32),
                 jax.ShapeDtypeStruct((TOKENS, 8), F32)]
    return in_specs, out_specs, out_shape


def _tail_args(x, modl, ln_g, ln_b, w_router, b_router):
    wr = jnp.pad(w_router, ((0, 0), (0, LANES - N_EXPERTS)))
    br = jnp.concatenate([b_router, jnp.full((LANES - N_EXPERTS,), NEG_BIG, F32)]).reshape(1, LANES)
    return (x, modl, modl, modl, ln_g.reshape(1, -1), ln_b.reshape(1, -1), wr, br)


def _post0(o_a, o_b, w_out, x, modl, ln_g, ln_b, w_router, b_router):
    tm = ROW_TILE
    nst = SEQ // tm
    tin, tout, tshape = _tail_specs(tm)
    w = w_out.astype(BF16)
    full = lambda shape: pl.BlockSpec(shape, lambda i: (0,) * len(shape))
    return pl.pallas_call(
        _post0_kernel,
        grid=(TOKENS // tm,),
        in_specs=[pl.BlockSpec((tm, MLA_OUT), lambda i: (i, 0)),
                  pl.BlockSpec((tm, FNET_WIDTH), lambda i: (i % nst, i // nst)),
                  full((MLA_OUT, D_MODEL)), full((FNET_WIDTH, D_MODEL))] + tin,
        out_specs=tout,
        out_shape=tshape,
        compiler_params=_params("arbitrary"),
        name="mix_a_tail",
    )(o_a, o_b, w[:MLA_OUT], w[MLA_OUT:], *_tail_args(x, modl, ln_g, ln_b, w_router, b_router))


def _convin_kernel(x_ref, sc_ref, sh_ref, w_ref, gb_ref, u_ref):
    h = _ln(x_ref[...], MOD_EPS) * (1.0 + sc_ref[...]) + sh_ref[...]
    p = _dot(h.astype(BF16), w_ref[...])
    gb_ref[...] = p[:, :D_MODEL].astype(BF16)
    u_ref[...] = (p[:, D_MODEL:2 * D_MODEL] * p[:, 2 * D_MODEL:]).astype(BF16)


def _convin(x, modl, w_in):
    tm = ROW_TILE
    row = pl.BlockSpec((tm, D_MODEL), lambda i: (i, 0))
    return pl.pallas_call(
        _convin_kernel,
        grid=(TOKENS // tm,),
        in_specs=[row, _mod_spec(1, tm), _mod_spec(0, tm),
                  pl.BlockSpec((D_MODEL, 3 * D_MODEL), lambda i: (0, 0))],
        out_specs=[row, row],
        out_shape=[jax.ShapeDtypeStruct((TOKENS, D_MODEL), BF16)] * 2,
        compiler_params=_params("arbitrary"),
        name="conv_front",
    )(x, modl, modl, w_in.astype(BF16))


def _post1(u, gate_b, conv_w, w_out, x, modl, ln_g, ln_b, w_router, b_router):
    tm = ROW_TILE
    halo = 16
    tin, tout, tshape = _tail_specs(tm)
    nh = TOKENS // halo
    cw = jnp.zeros((8, D_MODEL), F32).at[:CONV_WIDTH].set(conv_w)
    row = pl.BlockSpec((tm, D_MODEL), lambda i: (i, 0))
    return pl.pallas_call(
        _post1_kernel,
        grid=(TOKENS // tm,),
        in_specs=[row,
                  pl.BlockSpec((halo, D_MODEL), lambda i: (jnp.maximum(i * (tm // halo) - 1, 0), 0)),
                  pl.BlockSpec((halo, D_MODEL),
                               lambda i: (jnp.minimum((i + 1) * (tm // halo), nh - 1), 0)),
                  row,
                  pl.BlockSpec((8, D_MODEL), lambda i: (0, 0)),
                  pl.BlockSpec((D_MODEL, D_MODEL), lambda i: (0, 0))] + tin,
        out_specs=tout,
        out_shape=tshape,
        compiler_params=_params("arbitrary"),
        name="conv_tail",
    )(u, u, u, gate_b, cw, w_out.astype(BF16),
      *_tail_args(x, modl, ln_g, ln_b, w_router, b_router))


def _lane_cumsum(v):
    lane = lax.broadcasted_iota(jnp.int32, v.shape, 1)
    s = 1
    while s < N_EXPERTS:
        v = v + jnp.where(lane >= s, pltpu.roll(v, s, 1), 0.0)
        s *= 2
    return v


def _rank_kernel(te_ref, dest_ref, meta_ref, cnt_ref, carry_ref, pstart_ref):
    ph = pl.program_id(0)
    i = pl.program_id(1)
    tb = te_ref.shape[0]
    lane = lax.broadcasted_iota(jnp.int32, (tb, LANES), 1)
    te = te_ref[...]
    onehot = [lane == te[:, kk:kk + 1] for kk in range(TOP_K)]
    msum = sum(oh.astype(F32) for oh in onehot)
    colsum = jnp.sum(msum, axis=0, keepdims=True)

    @pl.when((ph == 0) & (i == 0))
    def _():
        cnt_ref[...] = jnp.zeros_like(cnt_ref)

    @pl.when(ph == 0)
    def _():
        cnt_ref[...] += jnp.broadcast_to(colsum, cnt_ref.shape)

    @pl.when((ph == 1) & (i == 0))
    def _():
        cnt = cnt_ref[...]
        padded = jnp.floor((cnt + (MOE_BLOCK - 1)) * (1.0 / MOE_BLOCK)) * MOE_BLOCK
        pend = _lane_cumsum(padded)
        pstart_ref[...] = pend - padded
        carry_ref[...] = jnp.zeros_like(carry_ref)
        lane8 = lax.broadcasted_iota(jnp.int32, (8, LANES), 1)
        row8 = lax.broadcasted_iota(jnp.int32, (8, LANES), 0)
        thr = ((row8 * LANES + lane8) * MOE_BLOCK).astype(F32)
        blk = jnp.zeros((8, LANES), F32)
        for e in range(N_EXPERTS):
            pe = jnp.sum(jnp.where(lane8 == e, pend, 0.0), axis=1, keepdims=True)
            blk = blk + (pe <= thr).astype(F32)
        blk = jnp.minimum(blk, N_EXPERTS - 1.0)
        total = jnp.sum(jnp.where(lane8 == N_EXPERTS - 1, pend, 0.0), axis=1, keepdims=True)
        nact = jnp.broadcast_to(total * (1.0 / MOE_BLOCK), (8, LANES))
        info = jnp.where(row8 == 0, cnt, jnp.where(row8 == 1, pend, nact))
        meta_ref[0:8, :] = blk.astype(jnp.int32)
        meta_ref[8:16, :] = info.astype(jnp.int32)

    @pl.when(ph == 1)
    def _():
        r = lax.broadcasted_iota(jnp.int32, (tb, tb), 0)
        c = lax.broadcasted_iota(jnp.int32, (tb, tb), 1)
        lower = (c < r).astype(BF16)
        prefix = _dot(lower, msum.astype(BF16))
        base = prefix + carry_ref[0:1, :] + pstart_ref[0:1, :]
        lane8 = lax.broadcasted_iota(jnp.int32, (tb, 8), 1)
        dest = jnp.zeros((tb, 8), jnp.int32)
        for kk in range(TOP_K):
            dk = jnp.sum(jnp.where(onehot[kk], base, 0.0), axis=1, keepdims=True)
            dest = jnp.where(lane8 == kk, dk.astype(jnp.int32), dest)
        dest_ref[...] = dest
        carry_ref[...] += jnp.broadcast_to(colsum, carry_ref.shape)


def _rank(top_e):
    tb = RANK_TILE
    dest, meta = pl.pallas_call(
        _rank_kernel,
        grid=(2, TOKENS // tb),
        in_specs=[pl.BlockSpec((tb, 8), lambda ph, i: (i, 0))],
        out_specs=[pl.BlockSpec((tb, 8), lambda ph, i: (i * ph, 0)),
                   pl.BlockSpec((16, LANES), lambda ph, i: (0, 0))],
        out_shape=[jax.ShapeDtypeStruct((TOKENS, 8), jnp.int32),
                   jax.ShapeDtypeStruct((16, LANES), jnp.int32)],
        scratch_shapes=[pltpu.VMEM((8, LANES), F32)] * 3,
        compiler_params=_params("arbitrary", "arbitrary"),
        name="moe_rank",
    )(top_e)
    dest_flat = dest[:, :TOP_K].reshape(-1)
    block_e = meta[0:8].reshape(-1)[:MOE_NBLOCKS]
    pend = meta[9, :N_EXPERTS]
    nact = meta[10, 0:1]
    return dest_flat, block_e, pend, nact


def _dispatch_kernel(pend_ref, nact_ref, dest_ref, h_ref, xs_ref, zero_ref, sem):
    i = pl.program_id(0)
    tm = h_ref.shape[0]

    @pl.when(i == 0)
    def _():
        zero_ref[...] = jnp.zeros_like(zero_ref)

        def zero_block(start):
            start = pl.multiple_of(start, MOE_BLOCK)
            cp = pltpu.make_async_copy(zero_ref, xs_ref.at[pl.ds(start, MOE_BLOCK)], sem)
            cp.start()
            cp.wait()

        for e in range(N_EXPERTS):
            start = pend_ref[e] - MOE_BLOCK

            @pl.when(start >= (pend_ref[e - 1] if e else 0))
            def _():
                zero_block(start)

        def tail(j, carry):
            zero_block(j * MOE_BLOCK)
            return carry

        lax.fori_loop(nact_ref[0], MOE_NBLOCKS, tail, 0)

    def issue(r, carry):
        for kk in range(TOP_K):
            d = dest_ref[r * TOP_K + kk]
            pltpu.make_async_copy(h_ref.at[pl.ds(r, 1)], xs_ref.at[pl.ds(d, 1)], sem).start()
        return carry

    lax.fori_loop(0, tm, issue, 0)

    def drain(r, carry):
        pltpu.make_async_copy(h_ref.at[pl.ds(0, 1)], xs_ref.at[pl.ds(0, 1)], sem).wait()
        return carry

    lax.fori_loop(0, tm * TOP_K, drain, 0)


def _dispatch(h, dest_flat, pend, nact):
    tm = ROW_TILE
    return pl.pallas_call(
        _dispatch_kernel,
        grid_spec=pltpu.PrefetchScalarGridSpec(
            num_scalar_prefetch=2,
            grid=(TOKENS // tm,),
            in_specs=[pl.BlockSpec((tm * TOP_K,), lambda i, *_: (i,), memory_space=pltpu.SMEM),
                      pl.BlockSpec((tm, D_MODEL), lambda i, *_: (i, 0))],
            out_specs=pl.BlockSpec(memory_space=pl.ANY),
            scratch_shapes=[pltpu.VMEM((MOE_BLOCK, D_MODEL), F32), pltpu.SemaphoreType.DMA],
        ),
        out_shape=jax.ShapeDtypeStruct((MOE_ROWS, D_MODEL), F32),
        compiler_params=_params("arbitrary"),
        name="moe_dispatch",
    )(pend, nact, dest_flat, h)


def _moe_kernel(be_ref, nact_ref, xs_ref, wg_ref, wl_ref, wd_ref, bg_ref, bl_ref, bd_ref,
                ys_ref, wg_s, wl_s, wd_s):
    j = pl.program_id(0)

    @pl.when(j >= nact_ref[0])
    def _():
        ys_ref[...] = jnp.zeros_like(ys_ref)

    @pl.when(j < nact_ref[0])
    def _():
        e = be_ref[j]
        e_prev = be_ref[jnp.maximum(j - 1, 0)]

        @pl.when((j == 0) | (e != e_prev))
        def _():
            wg_s[...] = wg_ref[...].astype(BF16)
            wl_s[...] = wl_ref[...].astype(BF16)
            wd_s[...] = wd_ref[...].astype(BF16)

        x = xs_ref[...].astype(BF16)
        g = jnp.minimum(_dot(x, wg_s[...]) + bg_ref[...], SWIGLU_LIMIT)
        l = jnp.clip(_dot(x, wl_s[...]) + bl_ref[...], -SWIGLU_LIMIT, SWIGLU_LIMIT)
        a = g * jax.nn.sigmoid(SWIGLU_ALPHA * g) * (l + 1.0)
        ys_ref[...] = _dot(a.astype(BF16), wd_s[...]) + bd_ref[...]


def _moe_experts(xs, block_e, nact, w_glu, b_glu, w_lin, b_lin, w_down, b_down):
    bm = MOE_BLOCK
    blk = lambda j, be, na: (jnp.minimum(j, na[0] - 1), 0)
    wsel = lambda j, be, na: (be[jnp.minimum(j, na[0] - 1)], 0, 0)
    wspec = pl.BlockSpec((None, D_MODEL, D_MODEL), wsel)
    bspec = pl.BlockSpec((None, 1, D_MODEL), wsel)
    b3 = lambda b: b.reshape(N_EXPERTS, 1, D_MODEL)
    return pl.pallas_call(
        _moe_kernel,
        grid_spec=pltpu.PrefetchScalarGridSpec(
            num_scalar_prefetch=2,
            grid=(MOE_NBLOCKS,),
            in_specs=[pl.BlockSpec((bm, D_MODEL), blk), wspec, wspec, wspec, bspec, bspec, bspec],
            out_specs=pl.BlockSpec((bm, D_MODEL), lambda j, be, na: (j, 0)),
            scratch_shapes=[pltpu.VMEM((D_MODEL, D_MODEL), BF16)] * 3,
        ),
        out_shape=jax.ShapeDtypeStruct((MOE_ROWS, D_MODEL), F32),
        compiler_params=_params("arbitrary"),
        name="moe_experts",
    )(block_e, nact, xs, w_glu, w_lin, w_down, b3(b_glu), b3(b_lin), b3(b_down))


def _combine_kernel(dest_ref, ys_ref, gt_ref, x_ref, gf_ref, lng_ref, lnb_ref, xo_ref, buf_ref, sem):
    tm = x_ref.shape[0]

    def issue(r, carry):
        for kk in range(TOP_K):
            d = dest_ref[r * TOP_K + kk]
            pltpu.make_async_copy(ys_ref.at[pl.ds(d, 1)], buf_ref.at[kk, pl.ds(r, 1)], sem).start()
        return carry

    lax.fori_loop(0, tm, issue, 0)

    def drain(r, carry):
        pltpu.make_async_copy(ys_ref.at[pl.ds(0, 1)], buf_ref.at[0, pl.ds(0, 1)], sem).wait()
        return carry

    lax.fori_loop(0, tm * TOP_K, drain, 0)
    gt = gt_ref[...]
    o = gt[:, 0:1] * buf_ref[0]
    for kk in range(1, TOP_K):
        o = o + gt[:, kk:kk + 1] * buf_ref[kk]
    xo_ref[...] = (_ln(DEEPNORM_ALPHA * x_ref[...] + gf_ref[...] * o, LN_EPS) * lng_ref[...]
                   + lnb_ref[...])


def _combine(ys, dest_flat, gates, x, modl, ln_g, ln_b):
    tm = COMBINE_TILE
    row = pl.BlockSpec((tm, D_MODEL), lambda i: (i, 0))
    vec = pl.BlockSpec((1, D_MODEL), lambda i: (0, 0))
    return pl.pallas_call(
        _combine_kernel,
        grid=(TOKENS // tm,),
        in_specs=[pl.BlockSpec((tm * TOP_K,), lambda i: (i,), memory_space=pltpu.SMEM),
                  pl.BlockSpec(memory_space=pl.ANY),
                  pl.BlockSpec((tm, 8), lambda i: (i, 0)),
                  row, _mod_spec(5, tm), vec, vec],
        out_specs=row,
        out_shape=jax.ShapeDtypeStruct((TOKENS, D_MODEL), F32),
        scratch_shapes=[pltpu.VMEM((TOP_K, tm, D_MODEL), F32), pltpu.SemaphoreType.DMA],
        compiler_params=_params("arbitrary"),
        name="moe_combine",
    )(dest_flat, ys, gates, x, modl, ln_g.reshape(1, -1), ln_b.reshape(1, -1))


def _rope_table(positions):
    half = QK_ROPE // 2
    inv_freq = ROPE_THETA ** (-jnp.arange(0, QK_ROPE, 2, dtype=F32) / QK_ROPE)
    ang = positions.astype(F32)[..., None] * inv_freq
    cos, sin = jnp.cos(ang), jnp.sin(ang)
    z = lambda n: jnp.zeros(ang.shape[:-1] + (n,), F32)
    tail = HEAD_PAD - QK_NOPE - QK_ROPE
    c = jnp.concatenate([jnp.ones(ang.shape[:-1] + (QK_NOPE,), F32), cos, cos, z(tail)], -1)
    s1 = jnp.concatenate([z(QK_NOPE), -sin, z(half), z(tail)], -1)
    s2 = jnp.concatenate([z(QK_NOPE), z(half), sin, z(tail)], -1)
    return jnp.concatenate([c, s1, s2], -1).reshape(TOKENS, 3 * LANES)


def _moe_layer(l, x, h, top_e, gates, modl, ln_g, ln_b, w_glu, b_glu, w_lin, b_lin, w_down, b_down):
    dest_flat, block_e, pend, nact = _rank(top_e)
    xs = _dispatch(h, dest_flat, pend, nact)
    ys = _moe_experts(xs, block_e, nact, w_glu[l], b_glu[l], w_lin[l], b_lin[l], w_down[l], b_down[l])
    return _combine(ys, dest_flat, gates, x, modl, ln_g[l], ln_b[l])


def kernel(x, c, positions, ada_w, ada_b, ln_mix_g, ln_mix_b, ln_ffn_g, ln_ffn_b, mla_w_in, mla_q_norm_g, mla_kv_norm_g, mla_w_uq, mla_w_uk, mla_w_uv, mix_a_w_out, conv_w_in, conv_w, conv_w_out, moe_w_router, moe_b_router, moe_w_glu, moe_b_glu, moe_w_lin, moe_b_lin, moe_w_down, moe_b_down):
    mod = _modulation(c, ada_w, ada_b)
    rope_tab = _rope_table(positions)
    x = x.reshape(TOKENS, D_MODEL)
    for l in range(DEPTH):
        modl = mod[l]
        i = l // 2
        if l % 2 == 0:
            q, k, v, a_t, b_t = _pre0(x, modl, rope_tab, mla_w_in[i], mla_q_norm_g[i],
                                      mla_kv_norm_g[i], mla_w_uq[i], mla_w_uk[i], mla_w_uv[i])
            o_a = _attention(q, k, v)
            o_b = _fnet(a_t, b_t)
            x, h, top_e, gates = _post0(o_a, o_b, mix_a_w_out[i], x, modl, ln_mix_g[l], ln_mix_b[l],
                                        moe_w_router[l], moe_b_router[l])
        else:
            gate_b, u = _convin(x, modl, conv_w_in[i])
            x, h, top_e, gates = _post1(u, gate_b, conv_w[i], conv_w_out[i], x, modl, ln_mix_g[l],
                                        ln_mix_b[l], moe_w_router[l], moe_b_router[l])
        x = _moe_layer(l, x, h, top_e, gates, modl, ln_ffn_g, ln_ffn_b, moe_w_glu, moe_b_glu,
                       moe_w_lin, moe_b_lin, moe_w_down, moe_b_down)
    return x.reshape(BATCH, SEQ, D_MODEL)
```

```python
import functools
import math

import numpy as np
import jax
import jax.numpy as jnp
from jax import lax
from jax.experimental import pallas as pl
from jax.experimental.pallas import tpu as pltpu

D_MODEL = 1024
BATCH = 4
SEQ = 4096
DEPTH = 2
TOKENS = BATCH * SEQ

MLA_HEADS = 8
QK_NOPE = 64
QK_ROPE = 32
V_HEAD = 64
Q_RANK = 256
KV_RANK = 128
ROPE_THETA = 10000.0
MLA_OUT = MLA_HEADS * V_HEAD
FNET_GROUPS = 8
FNET_GROUP_DIM = 64
FNET_WIDTH = FNET_GROUPS * FNET_GROUP_DIM
CONV_WIDTH = 3
N_EXPERTS = 32
TOP_K = 4
SWIGLU_LIMIT = 7.0
SWIGLU_ALPHA = 1.702
DEEPNORM_ALPHA = (2 * DEPTH) ** 0.25
LN_EPS = 1e-5
MOD_EPS = 1e-6
RMS_EPS = 1e-6
QK_SCALE = (QK_NOPE + QK_ROPE) ** -0.5
LOG2E = math.log2(math.e)

LANES = 128
HEAD_PAD = 128
VMEM_LIMIT = 56 * 1024 * 1024

ROW_TILE = 256
ATT_Q_TILE = 256
FNET_M_TILE = 1024
FNET_K_TILE = 512
RANK_TILE = 512
MOE_BLOCK = 256
MOE_ROWS = TOKENS * TOP_K + N_EXPERTS * MOE_BLOCK
MOE_NBLOCKS = MOE_ROWS // MOE_BLOCK
COMBINE_TILE = 128
NEG_BIG = -1e30

BF16 = jnp.bfloat16
F32 = jnp.float32


def _params(*sem):
    return pltpu.CompilerParams(dimension_semantics=sem, vmem_limit_bytes=VMEM_LIMIT)


def _ln(v, eps):
    mu = jnp.mean(v, axis=-1, keepdims=True)
    d = v - mu
    var = jnp.mean(d * d, axis=-1, keepdims=True)
    return d * lax.rsqrt(var + eps)


def _rms(v, g):
    return v * lax.rsqrt(jnp.mean(v * v, axis=-1, keepdims=True) + RMS_EPS) * g


def _dot(a, b):
    return jnp.dot(a, b, preferred_element_type=F32)


def _mod_kernel(c_ref, w_ref, b_ref, o_ref):
    c = c_ref[...]
    cond = c * jax.nn.sigmoid(c)
    o_ref[...] = _dot(cond, w_ref[...]) + b_ref[...]


def _modulation(c, ada_w, ada_b):
    tn = 1536
    n = 6 * D_MODEL
    c8 = jnp.zeros((8, D_MODEL), F32).at[:BATCH].set(c)
    out = pl.pallas_call(
        _mod_kernel,
        grid=(DEPTH, n // tn),
        in_specs=[
            pl.BlockSpec((8, D_MODEL), lambda l, j: (0, 0)),
            pl.BlockSpec((None, D_MODEL, tn), lambda l, j: (l, 0, j)),
            pl.BlockSpec((None, 1, tn), lambda l, j: (l, 0, j)),
        ],
        out_specs=pl.BlockSpec((None, 8, tn), lambda l, j: (l, 0, j)),
        out_shape=jax.ShapeDtypeStruct((DEPTH, 8, n), F32),
        compiler_params=_params("arbitrary", "arbitrary"),
        name="adaln_mod",
    )(c8, ada_w, ada_b.reshape(DEPTH, 1, n))
    return out[:, :BATCH].reshape(DEPTH, BATCH, 6, 1, D_MODEL)


def _mod_spec(chunk, tile):
    return pl.BlockSpec((None, None, 1, D_MODEL),
                        lambda i: ((i * tile) // SEQ, chunk, 0, 0))


def _rope(t, tab):
    c, s1, s2 = tab[:, :LANES], tab[:, LANES:2 * LANES], tab[:, 2 * LANES:]
    outs = []
    for h in range(MLA_HEADS):
        th = t[:, h * HEAD_PAD:(h + 1) * HEAD_PAD]
        outs.append(th * c + pltpu.roll(th, HEAD_PAD - QK_ROPE // 2, 1) * s1
                    + pltpu.roll(th, QK_ROPE // 2, 1) * s2)
    return jnp.concatenate(outs, axis=1)


def _pre0_kernel(x_ref, sc_ref, sh_ref, rope_ref, win_ref, gq_ref, gkv_ref, wuq_ref, wk_ref,
                 wuv_ref, dft_ref, q_ref, k_ref, v_ref, a_ref, b_ref):
    h = _ln(x_ref[...], MOD_EPS) * (1.0 + sc_ref[...]) + sh_ref[...]
    p = _dot(h.astype(BF16), win_ref[...])
    c_q = p[:, :Q_RANK]
    c_kv = p[:, Q_RANK:Q_RANK + KV_RANK]
    u_f = p[:, Q_RANK + KV_RANK:Q_RANK + KV_RANK + FNET_WIDTH]
    k_r = p[:, Q_RANK + KV_RANK + FNET_WIDTH:]
    tab = rope_ref[...]
    q = _dot(_rms(c_q, gq_ref[...]).astype(BF16), wuq_ref[...]) * (QK_SCALE * LOG2E)
    q_ref[...] = _rope(q, tab).astype(BF16)
    ckv = _rms(c_kv, gkv_ref[...]).astype(BF16)
    kin = jnp.concatenate([ckv, k_r.astype(BF16)], axis=1)
    k_ref[...] = _rope(_dot(kin, wk_ref[...]), tab).astype(BF16)
    v_ref[...] = _dot(ckv, wuv_ref[...]).astype(BF16)
    ab = _dot(u_f.astype(BF16), dft_ref[...])
    a_ref[...] = ab[:, :FNET_WIDTH].astype(BF16)
    b_ref[...] = ab[:, FNET_WIDTH:].astype(BF16)


def _channel_dft():
    n = np.arange(FNET_GROUP_DIM)
    ang = 2.0 * np.pi * ((n[:, None] * n[None, :]) % FNET_GROUP_DIM) / FNET_GROUP_DIM
    norm = 1.0 / math.sqrt(SEQ * FNET_GROUP_DIM)
    eye = np.eye(FNET_GROUPS)
    cc = np.kron(eye, np.cos(ang) * norm)
    ss = np.kron(eye, np.sin(ang) * norm)
    return jnp.asarray(np.concatenate([cc, ss], axis=1), BF16)


def _pre0(x, modl, rope_tab, w_in, gq, gkv, w_uq, w_uk, w_uv):
    tm = ROW_TILE
    nq = Q_RANK + KV_RANK
    w_in_r = jnp.concatenate(
        [w_in[:, :nq], w_in[:, nq + QK_ROPE:], w_in[:, nq:nq + QK_ROPE],
         jnp.zeros((D_MODEL, LANES - QK_ROPE), F32)], axis=1).astype(BF16)
    wuq_p = jnp.pad(w_uq.reshape(Q_RANK, MLA_HEADS, QK_NOPE + QK_ROPE),
                    ((0, 0), (0, 0), (0, HEAD_PAD - QK_NOPE - QK_ROPE)))
    wuq_p = wuq_p.reshape(Q_RANK, MLA_HEADS * HEAD_PAD).astype(BF16)
    wuk_p = jnp.pad(w_uk.reshape(KV_RANK, MLA_HEADS, QK_NOPE),
                    ((0, 0), (0, 0), (0, HEAD_PAD - QK_NOPE))).reshape(KV_RANK, MLA_HEADS * HEAD_PAD)
    place = np.zeros((LANES, MLA_HEADS, HEAD_PAD), np.float32)
    for j in range(QK_ROPE):
        place[j, :, QK_NOPE + j] = 1.0
    wk_p = jnp.concatenate([wuk_p, jnp.asarray(place.reshape(LANES, -1))], axis=0).astype(BF16)
    full = lambda shape: pl.BlockSpec(shape, lambda i: (0,) * len(shape))
    nst = SEQ // tm
    wide = MLA_HEADS * HEAD_PAD
    return pl.pallas_call(
        _pre0_kernel,
        grid=(TOKENS // tm,),
        in_specs=[
            pl.BlockSpec((tm, D_MODEL), lambda i: (i, 0)),
            _mod_spec(1, tm), _mod_spec(0, tm),
            pl.BlockSpec((tm, 3 * LANES), lambda i: (i, 0)),
            full((D_MODEL, D_MODEL)), full((1, Q_RANK)), full((1, KV_RANK)),
            full((Q_RANK, wide)), full((2 * LANES, wide)), full((KV_RANK, MLA_OUT)),
            full((FNET_WIDTH, 2 * FNET_WIDTH)),
        ],
        out_specs=[
            pl.BlockSpec((tm, wide), lambda i: (i, 0)),
            pl.BlockSpec((tm, wide), lambda i: (i, 0)),
            pl.BlockSpec((tm, MLA_OUT), lambda i: (i, 0)),
            pl.BlockSpec((tm, FNET_WIDTH), lambda i: (i % nst, i // nst)),
            pl.BlockSpec((tm, FNET_WIDTH), lambda i: (i % nst, i // nst)),
        ],
        out_shape=[
            jax.ShapeDtypeStruct((TOKENS, wide), BF16),
            jax.ShapeDtypeStruct((TOKENS, wide), BF16),
            jax.ShapeDtypeStruct((TOKENS, MLA_OUT), BF16),
            jax.ShapeDtypeStruct((SEQ, BATCH * FNET_WIDTH), BF16),
            jax.ShapeDtypeStruct((SEQ, BATCH * FNET_WIDTH), BF16),
        ],
        compiler_params=_params("arbitrary"),
        name="mla_fnet_front",
    )(x, modl, modl, rope_tab, w_in_r, gq.reshape(1, -1), gkv.reshape(1, -1), wuq_p, wk_p,
      w_uv.astype(BF16), _channel_dft())


def _attn_kernel(q_ref, k_ref, v_ref, o_ref):
    outs = []
    for hh in range(2):
        q = q_ref[:, hh * HEAD_PAD:(hh + 1) * HEAD_PAD]
        k = k_ref[:, hh * HEAD_PAD:(hh + 1) * HEAD_PAD]
        s = lax.dot_general(q, k, (((1,), (1,)), ((), ())), preferred_element_type=F32)
        m = jnp.max(s, axis=-1, keepdims=True)
        p = jnp.exp2(s - m)
        l = jnp.sum(p, axis=-1, keepdims=True)
        outs.append(_dot(p.astype(BF16), v_ref[...]) / l)
    lane = lax.broadcasted_iota(jnp.int32, outs[0].shape, 1)
    o_ref[...] = jnp.where(lane < V_HEAD, outs[0], outs[1]).astype(BF16)


def _attention(q, k, v):
    tq = ATT_Q_TILE
    wide = MLA_HEADS * HEAD_PAD
    q = q.reshape(BATCH, SEQ, wide)
    k = k.reshape(BATCH, SEQ, wide)
    v = v.reshape(BATCH, SEQ, MLA_OUT)
    out = pl.pallas_call(
        _attn_kernel,
        grid=(BATCH, MLA_HEADS // 2, SEQ // tq),
        in_specs=[
            pl.BlockSpec((None, tq, 2 * HEAD_PAD), lambda b, h, i: (b, i, h)),
            pl.BlockSpec((None, SEQ, 2 * HEAD_PAD), lambda b, h, i: (b, 0, h)),
            pl.BlockSpec((None, SEQ, 2 * V_HEAD), lambda b, h, i: (b, 0, h)),
        ],
        out_specs=pl.BlockSpec((None, tq, 2 * V_HEAD), lambda b, h, i: (b, i, h)),
        out_shape=jax.ShapeDtypeStruct((BATCH, SEQ, MLA_OUT), BF16),
        compiler_params=_params("arbitrary", "arbitrary", "arbitrary"),
        name="mla_attention",
    )(q, k, v)
    return out.reshape(TOKENS, MLA_OUT)


def _fnet_kernel(c_ref, s_ref, a_ref, b_ref, o_ref, acc_ref):
    kk = pl.program_id(1)

    @pl.when(kk == 0)
    def _():
        acc_ref[...] = jnp.zeros_like(acc_ref)

    acc_ref[...] += _dot(c_ref[...], a_ref[...]) + _dot(s_ref[...], b_ref[...])

    @pl.when(kk == pl.num_programs(1) - 1)
    def _():
        o_ref[...] = acc_ref[...].astype(BF16)


def _sequence_dft():
    r = int(math.isqrt(SEQ))
    k = jnp.arange(SEQ, dtype=jnp.int32)[:, None]
    j = jnp.arange(r, dtype=jnp.int32)[None, :]
    alpha = (2.0 * np.pi / r) * ((k * j) % r).astype(F32)
    beta = (2.0 * np.pi / SEQ) * ((k * j) % SEQ).astype(F32)
    ca, sa, cb, sb = jnp.cos(alpha), jnp.sin(alpha), jnp.cos(beta), jnp.sin(beta)
    cs = ca[:, :, None] * cb[:, None, :] - sa[:, :, None] * sb[:, None, :]
    sn = -(sa[:, :, None] * cb[:, None, :] + ca[:, :, None] * sb[:, None, :])
    return cs.reshape(SEQ, SEQ).astype(BF16), sn.reshape(SEQ, SEQ).astype(BF16)


def _fnet(a_t, b_t):
    tm, tk = FNET_M_TILE, FNET_K_TILE
    n = BATCH * FNET_WIDTH
    cs, sn = _sequence_dft()
    return pl.pallas_call(
        _fnet_kernel,
        grid=(SEQ // tm, SEQ // tk),
        in_specs=[
            pl.BlockSpec((tm, tk), lambda i, kk: (i, kk)),
            pl.BlockSpec((tm, tk), lambda i, kk: (i, kk)),
            pl.BlockSpec((tk, n), lambda i, kk: (kk, 0)),
            pl.BlockSpec((tk, n), lambda i, kk: (kk, 0)),
        ],
        out_specs=pl.BlockSpec((tm, n), lambda i, kk: (i, 0)),
        out_shape=jax.ShapeDtypeStruct((SEQ, n), BF16),
        scratch_shapes=[pltpu.VMEM((tm, n), F32)],
        compiler_params=_params("arbitrary", "arbitrary"),
        name="fnet_seq_dft",
    )(cs, sn, a_t, b_t)


def _top4(logits):
    rows = logits.shape[0]
    lane = lax.broadcasted_iota(jnp.int32, (rows, LANES), 1).astype(F32)
    work = logits
    vals, idxs = [], []
    for _ in range(TOP_K):
        m = jnp.max(work, axis=-1, keepdims=True)
        idx = jnp.min(jnp.where(work == m, lane, float(LANES)), axis=-1, keepdims=True)
        vals.append(m)
        idxs.append(idx.astype(jnp.int32))
        work = jnp.where(lane == idx, -jnp.inf, work)
    es = [jnp.exp(v - vals[0]) for v in vals]
    den = es[0] + es[1] + es[2] + es[3]
    lane8 = lax.broadcasted_iota(jnp.int32, (rows, 8), 1)
    top_e = jnp.zeros((rows, 8), jnp.int32)
    gates = jnp.zeros((rows, 8), F32)
    for kk in range(TOP_K):
        top_e = jnp.where(lane8 == kk, idxs[kk], top_e)
        gates = jnp.where(lane8 == kk, es[kk] / den, gates)
    return top_e, gates


def _mixer_tail(o, x_ref, gm_ref, scf_ref, shf_ref, lng_ref, lnb_ref, wr_ref, br_ref,
                xo_ref, h_ref, te_ref, gt_ref):
    xn = _ln(DEEPNORM_ALPHA * x_ref[...] + gm_ref[...] * o, LN_EPS) * lng_ref[...] + lnb_ref[...]
    xo_ref[...] = xn
    h = _ln(xn, MOD_EPS) * (1.0 + scf_ref[...]) + shf_ref[...]
    h_ref[...] = h
    logits = _dot(h, wr_ref[...]) + br_ref[...]
    top_e, gates = _top4(logits)
    te_ref[...] = top_e
    gt_ref[...] = gates


def _post0_kernel(oa_ref, ob_ref, wa_ref, wb_ref, *rest):
    o = _dot(oa_ref[...], wa_ref[...]) + _dot(ob_ref[...], wb_ref[...])
    _mixer_tail(o, *rest)


def _post1_kernel(u_ref, up_ref, un_ref, gb_ref, cw_ref, wo_ref, *rest):
    tm = u_ref.shape[0]
    i = pl.program_id(0)
    s0 = (i * tm) % SEQ
    u = u_ref[...].astype(F32)
    prev_row = jnp.where(s0 > 0, up_ref[...].astype(F32)[15:16, :], 0.0)
    next_row = jnp.where(s0 + tm < SEQ, un_ref[...].astype(F32)[0:1, :], 0.0)
    row = lax.broadcasted_iota(jnp.int32, u.shape, 0)
    u_m1 = jnp.where(row == 0, prev_row, pltpu.roll(u, 1, 0))
    u_p1 = jnp.where(row == tm - 1, next_row, pltpu.roll(u, tm - 1, 0))
    cw = cw_ref[...]
    y = u_m1 * cw[0:1, :] + u * cw[1:2, :] + u_p1 * cw[2:3, :]
    g = gb_ref[...].astype(F32) * y
    o = _dot(g.astype(BF16), wo_ref[...])
    _mixer_tail(o, *rest)


def _tail_specs(tm):
    row = lambda w: pl.BlockSpec((tm, w), lambda i: (i, 0))
    full = lambda shape: pl.BlockSpec(shape, lambda i: (0,) * len(shape))
    in_specs = [row(D_MODEL), _mod_spec(2, tm), _mod_spec(4, tm), _mod_spec(3, tm),
                full((1, D_MODEL)), full((1, D_MODEL)), full((D_MODEL, LANES)), full((1, LANES))]
    out_specs = [row(D_MODEL), row(D_MODEL), row(8), row(8)]
    out_shape = [jax.ShapeDtypeStruct((TOKENS, D_MODEL), F32),
                 jax.ShapeDtypeStruct((TOKENS, D_MODEL), F32),
                 jax.ShapeDtypeStruct((TOKENS, 8), jnp.int32),
                 jax.ShapeDtypeStruct((TOKENS, 8), F32)]
    return in_specs, out_specs, out_shape


def _tail_args(x, modl, ln_g, ln_b, w_router, b_router):
    wr = jnp.pad(w_router, ((0, 0), (0, LANES - N_EXPERTS)))
    br = jnp.concatenate([b_router, jnp.full((LANES - N_EXPERTS,), NEG_BIG, F32)]).reshape(1, LANES)
    return (x, modl, modl, modl, ln_g.reshape(1, -1), ln_b.reshape(1, -1), wr, br)


def _post0(o_a, o_b, w_out, x, modl, ln_g, ln_b, w_router, b_router):
    tm = ROW_TILE
    nst = SEQ // tm
    tin, tout, tshape = _tail_specs(tm)
    w = w_out.astype(BF16)
    full = lambda shape: pl.BlockSpec(shape, lambda i: (0,) * len(shape))
    return pl.pallas_call(
        _post0_kernel,
        grid=(TOKENS // tm,),
        in_specs=[pl.BlockSpec((tm, MLA_OUT), lambda i: (i, 0)),
                  pl.BlockSpec((tm, FNET_WIDTH), lambda i: (i % nst, i // nst)),
                  full((MLA_OUT, D_MODEL)), full((FNET_WIDTH, D_MODEL))] + tin,
        out_specs=tout,
        out_shape=tshape,
        compiler_params=_params("arbitrary"),
        name="mix_a_tail",
    )(o_a, o_b, w[:MLA_OUT], w[MLA_OUT:], *_tail_args(x, modl, ln_g, ln_b, w_router, b_router))


def _convin_kernel(x_ref, sc_ref, sh_ref, w_ref, gb_ref, u_ref):
    h = _ln(x_ref[...], MOD_EPS) * (1.0 + sc_ref[...]) + sh_ref[...]
    p = _dot(h.astype(BF16), w_ref[...])
    gb_ref[...] = p[:, :D_MODEL].astype(BF16)
    u_ref[...] = (p[:, D_MODEL:2 * D_MODEL] * p[:, 2 * D_MODEL:]).astype(BF16)


def _convin(x, modl, w_in):
    tm = ROW_TILE
    row = pl.BlockSpec((tm, D_MODEL), lambda i: (i, 0))
    return pl.pallas_call(
        _convin_kernel,
        grid=(TOKENS // tm,),
        in_specs=[row, _mod_spec(1, tm), _mod_spec(0, tm),
                  pl.BlockSpec((D_MODEL, 3 * D_MODEL), lambda i: (0, 0))],
        out_specs=[row, row],
        out_shape=[jax.ShapeDtypeStruct((TOKENS, D_MODEL), BF16)] * 2,
        compiler_params=_params("arbitrary"),
        name="conv_front",
    )(x, modl, modl, w_in.astype(BF16))


def _post1(u, gate_b, conv_w, w_out, x, modl, ln_g, ln_b, w_router, b_router):
    tm = ROW_TILE
    halo = 16
    tin, tout, tshape = _tail_specs(tm)
    nh = TOKENS // halo
    cw = jnp.zeros((8, D_MODEL), F32).at[:CONV_WIDTH].set(conv_w)
    row = pl.BlockSpec((tm, D_MODEL), lambda i: (i, 0))
    return pl.pallas_call(
        _post1_kernel,
        grid=(TOKENS // tm,),
        in_specs=[row,
                  pl.BlockSpec((halo, D_MODEL), lambda i: (jnp.maximum(i * (tm // halo) - 1, 0), 0)),
                  pl.BlockSpec((halo, D_MODEL),
                               lambda i: (jnp.minimum((i + 1) * (tm // halo), nh - 1), 0)),
                  row,
                  pl.BlockSpec((8, D_MODEL), lambda i: (0, 0)),
                  pl.BlockSpec((D_MODEL, D_MODEL), lambda i: (0, 0))] + tin,
        out_specs=tout,
        out_shape=tshape,
        compiler_params=_params("arbitrary"),
        name="conv_tail",
    )(u, u, u, gate_b, cw, w_out.astype(BF16),
      *_tail_args(x, modl, ln_g, ln_b, w_router, b_router))


def _lane_cumsum(v):
    lane = lax.broadcasted_iota(jnp.int32, v.shape, 1)
    s = 1
    while s < N_EXPERTS:
        v = v + jnp.where(lane >= s, pltpu.roll(v, s, 1), 0.0)
        s *= 2
    return v


def _rank_kernel(te_ref, dest_ref, meta_ref, cnt_ref, carry_ref, pstart_ref):
    ph = pl.program_id(0)
    i = pl.program_id(1)
    tb = te_ref.shape[0]
    lane = lax.broadcasted_iota(jnp.int32, (tb, LANES), 1)
    te = te_ref[...]
    onehot = [lane == te[:, kk:kk + 1] for kk in range(TOP_K)]
    msum = sum(oh.astype(F32) for oh in onehot)
    colsum = jnp.sum(msum, axis=0, keepdims=True)

    @pl.when((ph == 0) & (i == 0))
    def _():
        cnt_ref[...] = jnp.zeros_like(cnt_ref)

    @pl.when(ph == 0)
    def _():
        cnt_ref[...] += jnp.broadcast_to(colsum, cnt_ref.shape)

    @pl.when((ph == 1) & (i == 0))
    def _():
        cnt = cnt_ref[...]
        padded = jnp.floor((cnt + (MOE_BLOCK - 1)) * (1.0 / MOE_BLOCK)) * MOE_BLOCK
        pend = _lane_cumsum(padded)
        pstart_ref[...] = pend - padded
        carry_ref[...] = jnp.zeros_like(carry_ref)
        lane8 = lax.broadcasted_iota(jnp.int32, (8, LANES), 1)
        row8 = lax.broadcasted_iota(jnp.int32, (8, LANES), 0)
        thr = ((row8 * LANES + lane8) * MOE_BLOCK).astype(F32)
        blk = jnp.zeros((8, LANES), F32)
        for e in range(N_EXPERTS):
            pe = jnp.sum(jnp.where(lane8 == e, pend, 0.0), axis=1, keepdims=True)
            blk = blk + (pe <= thr).astype(F32)
        blk = jnp.minimum(blk, N_EXPERTS - 1.0)
        total = jnp.sum(jnp.where(lane8 == N_EXPERTS - 1, pend, 0.0), axis=1, keepdims=True)
        nact = jnp.broadcast_to(total * (1.0 / MOE_BLOCK), (8, LANES))
        info = jnp.where(row8 == 0, cnt, jnp.where(row8 == 1, pend, nact))
        meta_ref[0:8, :] = blk.astype(jnp.int32)
        meta_ref[8:16, :] = info.astype(jnp.int32)

    @pl.when(ph == 1)
    def _():
        r = lax.broadcasted_iota(jnp.int32, (tb, tb), 0)
        c = lax.broadcasted_iota(jnp.int32, (tb, tb), 1)
        lower = (c < r).astype(BF16)
        prefix = _dot(lower, msum.astype(BF16))
        base = prefix + carry_ref[0:1, :] + pstart_ref[0:1, :]
        lane8 = lax.broadcasted_iota(jnp.int32, (tb, 8), 1)
        dest = jnp.zeros((tb, 8), jnp.int32)
        for kk in range(TOP_K):
            dk = jnp.sum(jnp.where(onehot[kk], base, 0.0), axis=1, keepdims=True)
            dest = jnp.where(lane8 == kk, dk.astype(jnp.int32), dest)
        dest_ref[...] = dest
        carry_ref[...] += jnp.broadcast_to(colsum, carry_ref.shape)


def _rank(top_e):
    tb = RANK_TILE
    dest, meta = pl.pallas_call(
        _rank_kernel,
        grid=(2, TOKENS // tb),
        in_specs=[pl.BlockSpec((tb, 8), lambda ph, i: (i, 0))],
        out_specs=[pl.BlockSpec((tb, 8), lambda ph, i: (i * ph, 0)),
                   pl.BlockSpec((16, LANES), lambda ph, i: (0, 0))],
        out_shape=[jax.ShapeDtypeStruct((TOKENS, 8), jnp.int32),
                   jax.ShapeDtypeStruct((16, LANES), jnp.int32)],
        scratch_shapes=[pltpu.VMEM((8, LANES), F32)] * 3,
        compiler_params=_params("arbitrary", "arbitrary"),
        name="moe_rank",
    )(top_e)
    dest_flat = dest[:, :TOP_K].reshape(-1)
    block_e = meta[0:8].reshape(-1)[:MOE_NBLOCKS]
    pend = meta[9, :N_EXPERTS]
    nact = meta[10, 0:1]
    return dest_flat, block_e, pend, nact


def _dispatch_kernel(pend_ref, nact_ref, dest_ref, h_ref, xs_ref, zero_ref, sem):
    i = pl.program_id(0)
    tm = h_ref.shape[0]

    @pl.when(i == 0)
    def _():
        zero_ref[...] = jnp.zeros_like(zero_ref)

        def zero_block(start):
            start = pl.multiple_of(start, MOE_BLOCK)
            cp = pltpu.make_async_copy(zero_ref, xs_ref.at[pl.ds(start, MOE_BLOCK)], sem)
            cp.start()
            cp.wait()

        for e in range(N_EXPERTS):
            start = pend_ref[e] - MOE_BLOCK

            @pl.when(start >= (pend_ref[e - 1] if e else 0))
            def _():
                zero_block(start)

        def tail(j, carry):
            zero_block(j * MOE_BLOCK)
            return carry

        lax.fori_loop(nact_ref[0], MOE_NBLOCKS, tail, 0)

    def issue(r, carry):
        for kk in range(TOP_K):
            d = dest_ref[r * TOP_K + kk]
            pltpu.make_async_copy(h_ref.at[pl.ds(r, 1)], xs_ref.at[pl.ds(d, 1)], sem).start()
        return carry

    lax.fori_loop(0, tm, issue, 0, unroll=8)
    for kk in range(TOP_K):
        pltpu.make_async_copy(h_ref, xs_ref.at[pl.ds(0, tm)], sem).wait()


def _dispatch(h, dest_flat, pend, nact):
    tm = ROW_TILE
    return pl.pallas_call(
        _dispatch_kernel,
        grid_spec=pltpu.PrefetchScalarGridSpec(
            num_scalar_prefetch=2,
            grid=(TOKENS // tm,),
            in_specs=[pl.BlockSpec((tm * TOP_K,), lambda i, *_: (i,), memory_space=pltpu.SMEM),
                      pl.BlockSpec((tm, D_MODEL), lambda i, *_: (i, 0))],
            out_specs=pl.BlockSpec(memory_space=pl.ANY),
            scratch_shapes=[pltpu.VMEM((MOE_BLOCK, D_MODEL), F32), pltpu.SemaphoreType.DMA],
        ),
        out_shape=jax.ShapeDtypeStruct((MOE_ROWS, D_MODEL), F32),
        compiler_params=_params("arbitrary"),
        name="moe_dispatch",
    )(pend, nact, dest_flat, h)


def _moe_kernel(be_ref, nact_ref, xs_ref, wg_ref, wl_ref, wd_ref, bg_ref, bl_ref, bd_ref,
                ys_ref, wg_s, wl_s, wd_s):
    j = pl.program_id(0)

    @pl.when(j >= nact_ref[0])
    def _():
        ys_ref[...] = jnp.zeros_like(ys_ref)

    @pl.when(j < nact_ref[0])
    def _():
        e = be_ref[j]
        e_prev = be_ref[jnp.maximum(j - 1, 0)]

        @pl.when((j == 0) | (e != e_prev))
        def _():
            wg_s[...] = wg_ref[...].astype(BF16)
            wl_s[...] = wl_ref[...].astype(BF16)
            wd_s[...] = wd_ref[...].astype(BF16)

        x = xs_ref[...].astype(BF16)
        g = jnp.minimum(_dot(x, wg_s[...]) + bg_ref[...], SWIGLU_LIMIT)
        l = jnp.clip(_dot(x, wl_s[...]) + bl_ref[...], -SWIGLU_LIMIT, SWIGLU_LIMIT)
        a = g * jax.nn.sigmoid(SWIGLU_ALPHA * g) * (l + 1.0)
        ys_ref[...] = _dot(a.astype(BF16), wd_s[...]) + bd_ref[...]


def _moe_experts(l, xs, block_e, nact, w_glu, b_glu, w_lin, b_lin, w_down, b_down):
    bm = MOE_BLOCK
    blk = lambda j, be, na: (jnp.minimum(j, na[0] - 1), 0)
    wsel = lambda j, be, na: (l, be[jnp.minimum(j, na[0] - 1)], 0, 0)
    wspec = pl.BlockSpec((None, None, D_MODEL, D_MODEL), wsel)
    bspec = pl.BlockSpec((None, None, 1, D_MODEL), wsel)
    b3 = lambda b: b.reshape(DEPTH, N_EXPERTS, 1, D_MODEL)
    return pl.pallas_call(
        _moe_kernel,
        grid_spec=pltpu.PrefetchScalarGridSpec(
            num_scalar_prefetch=2,
            grid=(MOE_NBLOCKS,),
            in_specs=[pl.BlockSpec((bm, D_MODEL), blk), wspec, wspec, wspec, bspec, bspec, bspec],
            out_specs=pl.BlockSpec((bm, D_MODEL), lambda j, be, na: (j, 0)),
            scratch_shapes=[pltpu.VMEM((D_MODEL, D_MODEL), BF16)] * 3,
        ),
        out_shape=jax.ShapeDtypeStruct((MOE_ROWS, D_MODEL), F32),
        compiler_params=_params("arbitrary"),
        name="moe_experts",
    )(block_e, nact, xs, w_glu, w_lin, w_down, b3(b_glu), b3(b_lin), b3(b_down))


def _combine_kernel(dest_ref, ys_ref, gt_ref, x_ref, gf_ref, lng_ref, lnb_ref, xo_ref, buf_ref, sem):
    tm = x_ref.shape[0]

    def issue(r, carry):
        for kk in range(TOP_K):
            d = dest_ref[r * TOP_K + kk]
            pltpu.make_async_copy(ys_ref.at[pl.ds(d, 1)], buf_ref.at[kk, pl.ds(r, 1)], sem).start()
        return carry

    lax.fori_loop(0, tm, issue, 0, unroll=8)
    for kk in range(TOP_K):
        pltpu.make_async_copy(ys_ref.at[pl.ds(0, tm)], buf_ref.at[kk], sem).wait()
    gt = gt_ref[...]
    o = gt[:, 0:1] * buf_ref[0]
    for kk in range(1, TOP_K):
        o = o + gt[:, kk:kk + 1] * buf_ref[kk]
    xo_ref[...] = (_ln(DEEPNORM_ALPHA * x_ref[...] + gf_ref[...] * o, LN_EPS) * lng_ref[...]
                   + lnb_ref[...])


def _combine(ys, dest_flat, gates, x, modl, ln_g, ln_b):
    tm = COMBINE_TILE
    row = pl.BlockSpec((tm, D_MODEL), lambda i: (i, 0))
    vec = pl.BlockSpec((1, D_MODEL), lambda i: (0, 0))
    return pl.pallas_call(
        _combine_kernel,
        grid=(TOKENS // tm,),
        in_specs=[pl.BlockSpec((tm * TOP_K,), lambda i: (i,), memory_space=pltpu.SMEM),
                  pl.BlockSpec(memory_space=pl.ANY),
                  pl.BlockSpec((tm, 8), lambda i: (i, 0)),
                  row, _mod_spec(5, tm), vec, vec],
        out_specs=row,
        out_shape=jax.ShapeDtypeStruct((TOKENS, D_MODEL), F32),
        scratch_shapes=[pltpu.VMEM((TOP_K, tm, D_MODEL), F32), pltpu.SemaphoreType.DMA],
        compiler_params=_params("arbitrary"),
        name="moe_combine",
    )(dest_flat, ys, gates, x, modl, ln_g.reshape(1, -1), ln_b.reshape(1, -1))


def _rope_table(positions):
    half = QK_ROPE // 2
    inv_freq = ROPE_THETA ** (-jnp.arange(0, QK_ROPE, 2, dtype=F32) / QK_ROPE)
    ang = positions.astype(F32)[..., None] * inv_freq
    cos, sin = jnp.cos(ang), jnp.sin(ang)
    z = lambda n: jnp.zeros(ang.shape[:-1] + (n,), F32)
    tail = HEAD_PAD - QK_NOPE - QK_ROPE
    c = jnp.concatenate([jnp.ones(ang.shape[:-1] + (QK_NOPE,), F32), cos, cos, z(tail)], -1)
    s1 = jnp.concatenate([z(QK_NOPE), -sin, z(half), z(tail)], -1)
    s2 = jnp.concatenate([z(QK_NOPE), z(half), sin, z(tail)], -1)
    return jnp.concatenate([c, s1, s2], -1).reshape(TOKENS, 3 * LANES)


def _moe_layer(l, x, h, top_e, gates, modl, ln_g, ln_b, w_glu, b_glu, w_lin, b_lin, w_down, b_down):
    dest_flat, block_e, pend, nact = _rank(top_e)
    xs = _dispatch(h, dest_flat, pend, nact)
    ys = _moe_experts(l, xs, block_e, nact, w_glu, b_glu, w_lin, b_lin, w_down, b_down)
    return _combine(ys, dest_flat, gates, x, modl, ln_g[l], ln_b[l])


def kernel(x, c, positions, ada_w, ada_b, ln_mix_g, ln_mix_b, ln_ffn_g, ln_ffn_b, mla_w_in, mla_q_norm_g, mla_kv_norm_g, mla_w_uq, mla_w_uk, mla_w_uv, mix_a_w_out, conv_w_in, conv_w, conv_w_out, moe_w_router, moe_b_router, moe_w_glu, moe_b_glu, moe_w_lin, moe_b_lin, moe_w_down, moe_b_down):
    mod = _modulation(c, ada_w, ada_b)
    rope_tab = _rope_table(positions)
    x = x.reshape(TOKENS, D_MODEL)
    for l in range(DEPTH):
        modl = mod[l]
        i = l // 2
        if l % 2 == 0:
            q, k, v, a_t, b_t = _pre0(x, modl, rope_tab, mla_w_in[i], mla_q_norm_g[i],
                                      mla_kv_norm_g[i], mla_w_uq[i], mla_w_uk[i], mla_w_uv[i])
            o_a = _attention(q, k, v)
            o_b = _fnet(a_t, b_t)
            x, h, top_e, gates = _post0(o_a, o_b, mix_a_w_out[i], x, modl, ln_mix_g[l], ln_mix_b[l],
                                        moe_w_router[l], moe_b_router[l])
        else:
            gate_b, u = _convin(x, modl, conv_w_in[i])
            x, h, top_e, gates = _post1(u, gate_b, conv_w[i], conv_w_out[i], x, modl, ln_mix_g[l],
                                        ln_mix_b[l], moe_w_router[l], moe_b_router[l])
        x = _moe_layer(l, x, h, top_e, gates, modl, ln_ffn_g, ln_ffn_b, moe_w_glu, moe_b_glu,
                       moe_w_lin, moe_b_lin, moe_w_down, moe_b_down)
    return x.reshape(BATCH, SEQ, D_MODEL)
```

```python
import functools
import math

import numpy as np
import jax
import jax.numpy as jnp
from jax import lax
from jax.experimental import pallas as pl
from jax.experimental.pallas import tpu as pltpu

D_MODEL = 1024
BATCH = 4
SEQ = 4096
DEPTH = 2
TOKENS = BATCH * SEQ

MLA_HEADS = 8
QK_NOPE = 64
QK_ROPE = 32
V_HEAD = 64
Q_RANK = 256
KV_RANK = 128
ROPE_THETA = 10000.0
MLA_OUT = MLA_HEADS * V_HEAD
FNET_GROUPS = 8
FNET_GROUP_DIM = 64
FNET_WIDTH = FNET_GROUPS * FNET_GROUP_DIM
CONV_WIDTH = 3
N_EXPERTS = 32
TOP_K = 4
SWIGLU_LIMIT = 7.0
SWIGLU_ALPHA = 1.702
DEEPNORM_ALPHA = (2 * DEPTH) ** 0.25
LN_EPS = 1e-5
MOD_EPS = 1e-6
RMS_EPS = 1e-6
QK_SCALE = (QK_NOPE + QK_ROPE) ** -0.5
LOG2E = math.log2(math.e)

LANES = 128
HEAD_PAD = 128
ROW_CHUNKS = D_MODEL // LANES
VMEM_LIMIT = 56 * 1024 * 1024

ROW_TILE = 256
ATT_Q_TILE = 256
FNET_M_TILE = 1024
FNET_K_TILE = 512
RANK_TILE = 512
MOE_BLOCK = 256
MOE_ROWS = TOKENS * TOP_K + N_EXPERTS * MOE_BLOCK
MOE_NBLOCKS = MOE_ROWS // MOE_BLOCK
NEG_BIG = -1e30

BF16 = jnp.bfloat16
F32 = jnp.float32


def _params(*sem):
    return pltpu.CompilerParams(dimension_semantics=sem, vmem_limit_bytes=VMEM_LIMIT)


def _ln(v, eps):
    mu = jnp.mean(v, axis=-1, keepdims=True)
    d = v - mu
    var = jnp.mean(d * d, axis=-1, keepdims=True)
    return d * lax.rsqrt(var + eps)


def _rms(v, g):
    return v * lax.rsqrt(jnp.mean(v * v, axis=-1, keepdims=True) + RMS_EPS) * g


def _dot(a, b):
    return jnp.dot(a, b, preferred_element_type=F32)


def _store_row_tiles(ref, base, val):
    rows = val.shape[0]
    for ch in range(ROW_CHUNKS):
        ref[pl.ds(base + ch, rows, stride=ROW_CHUNKS), :] = val[:, ch * LANES:(ch + 1) * LANES]


def _load_row_tiles(ref, base, rows):
    return jnp.concatenate(
        [ref[pl.ds(base + ch, rows, stride=ROW_CHUNKS), :] for ch in range(ROW_CHUNKS)], axis=1)


def _mod_kernel(c_ref, w_ref, b_ref, o_ref):
    c = c_ref[...]
    cond = c * jax.nn.sigmoid(c)
    o_ref[...] = _dot(cond, w_ref[...]) + b_ref[...]


def _modulation(c, ada_w, ada_b):
    tn = 1536
    n = 6 * D_MODEL
    c8 = jnp.zeros((8, D_MODEL), F32).at[:BATCH].set(c)
    out = pl.pallas_call(
        _mod_kernel,
        grid=(DEPTH, n // tn),
        in_specs=[
            pl.BlockSpec((8, D_MODEL), lambda l, j: (0, 0)),
            pl.BlockSpec((None, D_MODEL, tn), lambda l, j: (l, 0, j)),
            pl.BlockSpec((None, 1, tn), lambda l, j: (l, 0, j)),
        ],
        out_specs=pl.BlockSpec((None, 8, tn), lambda l, j: (l, 0, j)),
        out_shape=jax.ShapeDtypeStruct((DEPTH, 8, n), F32),
        compiler_params=_params("arbitrary", "arbitrary"),
        name="adaln_mod",
    )(c8, ada_w, ada_b.reshape(DEPTH, 1, n))
    return out[:, :BATCH].reshape(DEPTH, BATCH, 6, 1, D_MODEL)


def _mod_spec(chunk, tile):
    return pl.BlockSpec((None, None, 1, D_MODEL),
                        lambda i: ((i * tile) // SEQ, chunk, 0, 0))


def _rope(t, tab):
    c, s1, s2 = tab[:, :LANES], tab[:, LANES:2 * LANES], tab[:, 2 * LANES:]
    outs = []
    for h in range(MLA_HEADS):
        th = t[:, h * HEAD_PAD:(h + 1) * HEAD_PAD]
        outs.append(th * c + pltpu.roll(th, HEAD_PAD - QK_ROPE // 2, 1) * s1
                    + pltpu.roll(th, QK_ROPE // 2, 1) * s2)
    return jnp.concatenate(outs, axis=1)


def _pre0_kernel(x_ref, sc_ref, sh_ref, rope_ref, win_ref, gq_ref, gkv_ref, wuq_ref, wk_ref,
                 wuv_ref, dft_ref, q_ref, k_ref, v_ref, a_ref, b_ref):
    h = _ln(x_ref[...], MOD_EPS) * (1.0 + sc_ref[...]) + sh_ref[...]
    p = _dot(h.astype(BF16), win_ref[...])
    c_q = p[:, :Q_RANK]
    c_kv = p[:, Q_RANK:Q_RANK + KV_RANK]
    u_f = p[:, Q_RANK + KV_RANK:Q_RANK + KV_RANK + FNET_WIDTH]
    k_r = p[:, Q_RANK + KV_RANK + FNET_WIDTH:]
    tab = rope_ref[...]
    q = _dot(_rms(c_q, gq_ref[...]).astype(BF16), wuq_ref[...]) * (QK_SCALE * LOG2E)
    q_ref[...] = _rope(q, tab).astype(BF16)
    ckv = _rms(c_kv, gkv_ref[...]).astype(BF16)
    kin = jnp.concatenate([ckv, k_r.astype(BF16)], axis=1)
    k_ref[...] = _rope(_dot(kin, wk_ref[...]), tab).astype(BF16)
    v_ref[...] = _dot(ckv, wuv_ref[...]).astype(BF16)
    ab = _dot(u_f.astype(BF16), dft_ref[...])
    a_ref[...] = ab[:, :FNET_WIDTH].astype(BF16)
    b_ref[...] = ab[:, FNET_WIDTH:].astype(BF16)


def _channel_dft():
    n = np.arange(FNET_GROUP_DIM)
    ang = 2.0 * np.pi * ((n[:, None] * n[None, :]) % FNET_GROUP_DIM) / FNET_GROUP_DIM
    norm = 1.0 / math.sqrt(SEQ * FNET_GROUP_DIM)
    eye = np.eye(FNET_GROUPS)
    cc = np.kron(eye, np.cos(ang) * norm)
    ss = np.kron(eye, np.sin(ang) * norm)
    return jnp.asarray(np.concatenate([cc, ss], axis=1), BF16)


def _pre0(x, modl, rope_tab, w_in, gq, gkv, w_uq, w_uk, w_uv):
    tm = ROW_TILE
    nq = Q_RANK + KV_RANK
    w_in_r = jnp.concatenate(
        [w_in[:, :nq], w_in[:, nq + QK_ROPE:], w_in[:, nq:nq + QK_ROPE],
         jnp.zeros((D_MODEL, LANES - QK_ROPE), F32)], axis=1).astype(BF16)
    wuq_p = jnp.pad(w_uq.reshape(Q_RANK, MLA_HEADS, QK_NOPE + QK_ROPE),
                    ((0, 0), (0, 0), (0, HEAD_PAD - QK_NOPE - QK_ROPE)))
    wuq_p = wuq_p.reshape(Q_RANK, MLA_HEADS * HEAD_PAD).astype(BF16)
    wuk_p = jnp.pad(w_uk.reshape(KV_RANK, MLA_HEADS, QK_NOPE),
                    ((0, 0), (0, 0), (0, HEAD_PAD - QK_NOPE))).reshape(KV_RANK, MLA_HEADS * HEAD_PAD)
    place = np.zeros((LANES, MLA_HEADS, HEAD_PAD), np.float32)
    for j in range(QK_ROPE):
        place[j, :, QK_NOPE + j] = 1.0
    wk_p = jnp.concatenate([wuk_p, jnp.asarray(place.reshape(LANES, -1))], axis=0).astype(BF16)
    full = lambda shape: pl.BlockSpec(shape, lambda i: (0,) * len(shape))
    nst = SEQ // tm
    wide = MLA_HEADS * HEAD_PAD
    return pl.pallas_call(
        _pre0_kernel,
        grid=(TOKENS // tm,),
        in_specs=[
            pl.BlockSpec((tm, D_MODEL), lambda i: (i, 0)),
            _mod_spec(1, tm), _mod_spec(0, tm),
            pl.BlockSpec((tm, 3 * LANES), lambda i: (i, 0)),
            full((D_MODEL, D_MODEL)), full((1, Q_RANK)), full((1, KV_RANK)),
            full((Q_RANK, wide)), full((2 * LANES, wide)), full((KV_RANK, MLA_OUT)),
            full((FNET_WIDTH, 2 * FNET_WIDTH)),
        ],
        out_specs=[
            pl.BlockSpec((tm, wide), lambda i: (i, 0)),
            pl.BlockSpec((tm, wide), lambda i: (i, 0)),
            pl.BlockSpec((tm, MLA_OUT), lambda i: (i, 0)),
            pl.BlockSpec((tm, FNET_WIDTH), lambda i: (i % nst, i // nst)),
            pl.BlockSpec((tm, FNET_WIDTH), lambda i: (i % nst, i // nst)),
        ],
        out_shape=[
            jax.ShapeDtypeStruct((TOKENS, wide), BF16),
            jax.ShapeDtypeStruct((TOKENS, wide), BF16),
            jax.ShapeDtypeStruct((TOKENS, MLA_OUT), BF16),
            jax.ShapeDtypeStruct((SEQ, BATCH * FNET_WIDTH), BF16),
            jax.ShapeDtypeStruct((SEQ, BATCH * FNET_WIDTH), BF16),
        ],
        compiler_params=_params("arbitrary"),
        name="mla_fnet_front",
    )(x, modl, modl, rope_tab, w_in_r, gq.reshape(1, -1), gkv.reshape(1, -1), wuq_p, wk_p,
      w_uv.astype(BF16), _channel_dft())


def _attn_kernel(q_ref, k_ref, v_ref, o_ref):
    outs = []
    for hh in range(2):
        q = q_ref[:, hh * HEAD_PAD:(hh + 1) * HEAD_PAD]
        k = k_ref[:, hh * HEAD_PAD:(hh + 1) * HEAD_PAD]
        s = lax.dot_general(q, k, (((1,), (1,)), ((), ())), preferred_element_type=F32)
        m = jnp.max(s, axis=-1, keepdims=True)
        p = jnp.exp2(s - m)
        l = jnp.sum(p, axis=-1, keepdims=True)
        outs.append(_dot(p.astype(BF16), v_ref[...]) / l)
    lane = lax.broadcasted_iota(jnp.int32, outs[0].shape, 1)
    o_ref[...] = jnp.where(lane < V_HEAD, outs[0], outs[1]).astype(BF16)


def _attention(q, k, v):
    tq = ATT_Q_TILE
    wide = MLA_HEADS * HEAD_PAD
    q = q.reshape(BATCH, SEQ, wide)
    k = k.reshape(BATCH, SEQ, wide)
    v = v.reshape(BATCH, SEQ, MLA_OUT)
    out = pl.pallas_call(
        _attn_kernel,
        grid=(BATCH, MLA_HEADS // 2, SEQ // tq),
        in_specs=[
            pl.BlockSpec((None, tq, 2 * HEAD_PAD), lambda b, h, i: (b, i, h)),
            pl.BlockSpec((None, SEQ, 2 * HEAD_PAD), lambda b, h, i: (b, 0, h)),
            pl.BlockSpec((None, SEQ, 2 * V_HEAD), lambda b, h, i: (b, 0, h)),
        ],
        out_specs=pl.BlockSpec((None, tq, 2 * V_HEAD), lambda b, h, i: (b, i, h)),
        out_shape=jax.ShapeDtypeStruct((BATCH, SEQ, MLA_OUT), BF16),
        compiler_params=_params("arbitrary", "arbitrary", "arbitrary"),
        name="mla_attention",
    )(q, k, v)
    return out.reshape(TOKENS, MLA_OUT)


def _fnet_kernel(c_ref, s_ref, a_ref, b_ref, o_ref, acc_ref):
    kk = pl.program_id(1)

    @pl.when(kk == 0)
    def _():
        acc_ref[...] = jnp.zeros_like(acc_ref)

    acc_ref[...] += _dot(c_ref[...], a_ref[...]) + _dot(s_ref[...], b_ref[...])

    @pl.when(kk == pl.num_programs(1) - 1)
    def _():
        o_ref[...] = acc_ref[...].astype(BF16)


def _sequence_dft():
    r = int(math.isqrt(SEQ))
    k = jnp.arange(SEQ, dtype=jnp.int32)[:, None]
    j = jnp.arange(r, dtype=jnp.int32)[None, :]
    alpha = (2.0 * np.pi / r) * ((k * j) % r).astype(F32)
    beta = (2.0 * np.pi / SEQ) * ((k * j) % SEQ).astype(F32)
    ca, sa, cb, sb = jnp.cos(alpha), jnp.sin(alpha), jnp.cos(beta), jnp.sin(beta)
    cs = ca[:, :, None] * cb[:, None, :] - sa[:, :, None] * sb[:, None, :]
    sn = -(sa[:, :, None] * cb[:, None, :] + ca[:, :, None] * sb[:, None, :])
    return cs.reshape(SEQ, SEQ).astype(BF16), sn.reshape(SEQ, SEQ).astype(BF16)


def _fnet(a_t, b_t):
    tm, tk = FNET_M_TILE, FNET_K_TILE
    n = BATCH * FNET_WIDTH
    cs, sn = _sequence_dft()
    return pl.pallas_call(
        _fnet_kernel,
        grid=(SEQ // tm, SEQ // tk),
        in_specs=[
            pl.BlockSpec((tm, tk), lambda i, kk: (i, kk)),
            pl.BlockSpec((tm, tk), lambda i, kk: (i, kk)),
            pl.BlockSpec((tk, n), lambda i, kk: (kk, 0)),
            pl.BlockSpec((tk, n), lambda i, kk: (kk, 0)),
        ],
        out_specs=pl.BlockSpec((tm, n), lambda i, kk: (i, 0)),
        out_shape=jax.ShapeDtypeStruct((SEQ, n), BF16),
        scratch_shapes=[pltpu.VMEM((tm, n), F32)],
        compiler_params=_params("arbitrary", "arbitrary"),
        name="fnet_seq_dft",
    )(cs, sn, a_t, b_t)


def _top4(logits):
    rows = logits.shape[0]
    lane = lax.broadcasted_iota(jnp.int32, (rows, LANES), 1).astype(F32)
    work = logits
    vals, idxs = [], []
    for _ in range(TOP_K):
        m = jnp.max(work, axis=-1, keepdims=True)
        idx = jnp.min(jnp.where(work == m, lane, float(LANES)), axis=-1, keepdims=True)
        vals.append(m)
        idxs.append(idx.astype(jnp.int32))
        work = jnp.where(lane == idx, -jnp.inf, work)
    es = [jnp.exp(v - vals[0]) for v in vals]
    den = es[0] + es[1] + es[2] + es[3]
    lane8 = lax.broadcasted_iota(jnp.int32, (rows, 8), 1)
    top_e = jnp.zeros((rows, 8), jnp.int32)
    gates = jnp.zeros((rows, 8), F32)
    for kk in range(TOP_K):
        top_e = jnp.where(lane8 == kk, idxs[kk], top_e)
        gates = jnp.where(lane8 == kk, es[kk] / den, gates)
    return top_e, gates


def _mixer_tail(o, x_ref, gm_ref, scf_ref, shf_ref, lng_ref, lnb_ref, wr_ref, br_ref,
                xo_ref, h_ref, te_ref, gt_ref):
    xn = _ln(DEEPNORM_ALPHA * x_ref[...] + gm_ref[...] * o, LN_EPS) * lng_ref[...] + lnb_ref[...]
    xo_ref[...] = xn
    h = _ln(xn, MOD_EPS) * (1.0 + scf_ref[...]) + shf_ref[...]
    _store_row_tiles(h_ref, 0, h)
    logits = _dot(h, wr_ref[...]) + br_ref[...]
    top_e, gates = _top4(logits)
    te_ref[...] = top_e
    gt_ref[...] = gates


def _post0_kernel(oa_ref, ob_ref, wa_ref, wb_ref, *rest):
    o = _dot(oa_ref[...], wa_ref[...]) + _dot(ob_ref[...], wb_ref[...])
    _mixer_tail(o, *rest)


def _post1_kernel(u_ref, up_ref, un_ref, gb_ref, cw_ref, wo_ref, *rest):
    tm = u_ref.shape[0]
    i = pl.program_id(0)
    s0 = (i * tm) % SEQ
    u = u_ref[...].astype(F32)
    prev_row = jnp.where(s0 > 0, up_ref[...].astype(F32)[15:16, :], 0.0)
    next_row = jnp.where(s0 + tm < SEQ, un_ref[...].astype(F32)[0:1, :], 0.0)
    row = lax.broadcasted_iota(jnp.int32, u.shape, 0)
    u_m1 = jnp.where(row == 0, prev_row, pltpu.roll(u, 1, 0))
    u_p1 = jnp.where(row == tm - 1, next_row, pltpu.roll(u, tm - 1, 0))
    cw = cw_ref[...]
    y = u_m1 * cw[0:1, :] + u * cw[1:2, :] + u_p1 * cw[2:3, :]
    g = gb_ref[...].astype(F32) * y
    o = _dot(g.astype(BF16), wo_ref[...])
    _mixer_tail(o, *rest)


def _tail_specs(tm):
    row = lambda w: pl.BlockSpec((tm, w), lambda i: (i, 0))
    full = lambda shape: pl.BlockSpec(shape, lambda i: (0,) * len(shape))
    in_specs = [row(D_MODEL), _mod_spec(2, tm), _mod_spec(4, tm), _mod_spec(3, tm),
                full((1, D_MODEL)), full((1, D_MODEL)), full((D_MODEL, LANES)), full((1, LANES))]
    out_specs = [row(D_MODEL), pl.BlockSpec((tm * ROW_CHUNKS, LANES), lambda i: (i, 0)),
                 row(8), row(8)]
    out_shape = [jax.ShapeDtypeStruct((TOKENS, D_MODEL), F32),
                 jax.ShapeDtypeStruct((TOKENS * ROW_CHUNKS, LANES), F32),
                 jax.ShapeDtypeStruct((TOKENS, 8), jnp.int32),
                 jax.ShapeDtypeStruct((TOKENS, 8), F32)]
    return in_specs, out_specs, out_shape


def _tail_args(x, modl, ln_g, ln_b, w_router, b_router):
    wr = jnp.pad(w_router, ((0, 0), (0, LANES - N_EXPERTS)))
    br = jnp.concatenate([b_router, jnp.full((LANES - N_EXPERTS,), NEG_BIG, F32)]).reshape(1, LANES)
    return (x, modl, modl, modl, ln_g.reshape(1, -1), ln_b.reshape(1, -1), wr, br)


def _post0(o_a, o_b, w_out, x, modl, ln_g, ln_b, w_router, b_router):
    tm = ROW_TILE
    nst = SEQ // tm
    tin, tout, tshape = _tail_specs(tm)
    w = w_out.astype(BF16)
    full = lambda shape: pl.BlockSpec(shape, lambda i: (0,) * len(shape))
    return pl.pallas_call(
        _post0_kernel,
        grid=(TOKENS // tm,),
        in_specs=[pl.BlockSpec((tm, MLA_OUT), lambda i: (i, 0)),
                  pl.BlockSpec((tm, FNET_WIDTH), lambda i: (i % nst, i // nst)),
                  full((MLA_OUT, D_MODEL)), full((FNET_WIDTH, D_MODEL))] + tin,
        out_specs=tout,
        out_shape=tshape,
        compiler_params=_params("arbitrary"),
        name="mix_a_tail",
    )(o_a, o_b, w[:MLA_OUT], w[MLA_OUT:], *_tail_args(x, modl, ln_g, ln_b, w_router, b_router))


def _convin_kernel(x_ref, sc_ref, sh_ref, w_ref, gb_ref, u_ref):
    h = _ln(x_ref[...], MOD_EPS) * (1.0 + sc_ref[...]) + sh_ref[...]
    p = _dot(h.astype(BF16), w_ref[...])
    gb_ref[...] = p[:, :D_MODEL].astype(BF16)
    u_ref[...] = (p[:, D_MODEL:2 * D_MODEL] * p[:, 2 * D_MODEL:]).astype(BF16)


def _convin(x, modl, w_in):
    tm = ROW_TILE
    row = pl.BlockSpec((tm, D_MODEL), lambda i: (i, 0))
    return pl.pallas_call(
        _convin_kernel,
        grid=(TOKENS // tm,),
        in_specs=[row, _mod_spec(1, tm), _mod_spec(0, tm),
                  pl.BlockSpec((D_MODEL, 3 * D_MODEL), lambda i: (0, 0))],
        out_specs=[row, row],
        out_shape=[jax.ShapeDtypeStruct((TOKENS, D_MODEL), BF16)] * 2,
        compiler_params=_params("arbitrary"),
        name="conv_front",
    )(x, modl, modl, w_in.astype(BF16))


def _post1(u, gate_b, conv_w, w_out, x, modl, ln_g, ln_b, w_router, b_router):
    tm = ROW_TILE
    halo = 16
    tin, tout, tshape = _tail_specs(tm)
    nh = TOKENS // halo
    cw = jnp.zeros((8, D_MODEL), F32).at[:CONV_WIDTH].set(conv_w)
    row = pl.BlockSpec((tm, D_MODEL), lambda i: (i, 0))
    return pl.pallas_call(
        _post1_kernel,
        grid=(TOKENS // tm,),
        in_specs=[row,
                  pl.BlockSpec((halo, D_MODEL), lambda i: (jnp.maximum(i * (tm // halo) - 1, 0), 0)),
                  pl.BlockSpec((halo, D_MODEL),
                               lambda i: (jnp.minimum((i + 1) * (tm // halo), nh - 1), 0)),
                  row,
                  pl.BlockSpec((8, D_MODEL), lambda i: (0, 0)),
                  pl.BlockSpec((D_MODEL, D_MODEL), lambda i: (0, 0))] + tin,
        out_specs=tout,
        out_shape=tshape,
        compiler_params=_params("arbitrary"),
        name="conv_tail",
    )(u, u, u, gate_b, cw, w_out.astype(BF16),
      *_tail_args(x, modl, ln_g, ln_b, w_router, b_router))


def _lane_cumsum(v):
    lane = lax.broadcasted_iota(jnp.int32, v.shape, 1)
    s = 1
    while s < N_EXPERTS:
        v = v + jnp.where(lane >= s, pltpu.roll(v, s, 1), 0.0)
        s *= 2
    return v


def _rank_kernel(te_ref, dest_ref, meta_ref, cnt_ref, carry_ref, pstart_ref):
    ph = pl.program_id(0)
    i = pl.program_id(1)
    tb = te_ref.shape[0]
    lane = lax.broadcasted_iota(jnp.int32, (tb, LANES), 1)
    te = te_ref[...]
    onehot = [lane == te[:, kk:kk + 1] for kk in range(TOP_K)]
    msum = sum(oh.astype(F32) for oh in onehot)
    colsum = jnp.sum(msum, axis=0, keepdims=True)

    @pl.when((ph == 0) & (i == 0))
    def _():
        cnt_ref[...] = jnp.zeros_like(cnt_ref)

    @pl.when(ph == 0)
    def _():
        cnt_ref[...] += jnp.broadcast_to(colsum, cnt_ref.shape)

    @pl.when((ph == 1) & (i == 0))
    def _():
        cnt = cnt_ref[...]
        padded = jnp.floor((cnt + (MOE_BLOCK - 1)) * (1.0 / MOE_BLOCK)) * MOE_BLOCK
        pend = _lane_cumsum(padded)
        pstart_ref[...] = pend - padded
        carry_ref[...] = jnp.zeros_like(carry_ref)
        lane8 = lax.broadcasted_iota(jnp.int32, (8, LANES), 1)
        row8 = lax.broadcasted_iota(jnp.int32, (8, LANES), 0)
        thr = ((row8 * LANES + lane8) * MOE_BLOCK).astype(F32)
        blk = jnp.zeros((8, LANES), F32)
        for e in range(N_EXPERTS):
            pe = jnp.sum(jnp.where(lane8 == e, pend, 0.0), axis=1, keepdims=True)
            blk = blk + (pe <= thr).astype(F32)
        blk = jnp.minimum(blk, N_EXPERTS - 1.0)
        total = jnp.sum(jnp.where(lane8 == N_EXPERTS - 1, pend, 0.0), axis=1, keepdims=True)
        nact = jnp.broadcast_to(total * (1.0 / MOE_BLOCK), (8, LANES))
        info = jnp.where(row8 == 0, cnt, jnp.where(row8 == 1, pend, nact))
        meta_ref[0:8, :] = blk.astype(jnp.int32)
        meta_ref[8:16, :] = info.astype(jnp.int32)

    @pl.when(ph == 1)
    def _():
        r = lax.broadcasted_iota(jnp.int32, (tb, tb), 0)
        c = lax.broadcasted_iota(jnp.int32, (tb, tb), 1)
        lower = (c < r).astype(BF16)
        prefix = _dot(lower, msum.astype(BF16))
        base = prefix + carry_ref[0:1, :] + pstart_ref[0:1, :]
        lane8 = lax.broadcasted_iota(jnp.int32, (tb, 8), 1)
        dest = jnp.zeros((tb, 8), jnp.int32)
        for kk in range(TOP_K):
            dk = jnp.sum(jnp.where(onehot[kk], base, 0.0), axis=1, keepdims=True)
            dest = jnp.where(lane8 == kk, dk.astype(jnp.int32), dest)
        dest_ref[...] = dest
        carry_ref[...] += jnp.broadcast_to(colsum, carry_ref.shape)


def _rank(top_e):
    tb = RANK_TILE
    dest, meta = pl.pallas_call(
        _rank_kernel,
        grid=(2, TOKENS // tb),
        in_specs=[pl.BlockSpec((tb, 8), lambda ph, i: (i, 0))],
        out_specs=[pl.BlockSpec((tb, 8), lambda ph, i: (i * ph, 0)),
                   pl.BlockSpec((16, LANES), lambda ph, i: (0, 0))],
        out_shape=[jax.ShapeDtypeStruct((TOKENS, 8), jnp.int32),
                   jax.ShapeDtypeStruct((16, LANES), jnp.int32)],
        scratch_shapes=[pltpu.VMEM((8, LANES), F32)] * 3,
        compiler_params=_params("arbitrary", "arbitrary"),
        name="moe_rank",
    )(top_e)
    dest_flat = dest[:, :TOP_K].reshape(-1)
    block_e = meta[0:8].reshape(-1)[:MOE_NBLOCKS]
    counts = meta[8, :N_EXPERTS]
    pend = meta[9, :N_EXPERTS]
    nact = meta[10, 0:1]
    return dest_flat, block_e, counts, pend, nact


def _slot_kernel(pend_ref, cnt_ref, nact_ref, dest_ref, slot_ref):
    i = pl.program_id(0)
    tc = dest_ref.shape[0] // TOP_K

    def fill_pad(s, carry):
        slot_ref[s] = TOP_K * TOKENS + (s & (2 * MOE_BLOCK - 1))
        return carry

    @pl.when(i == 0)
    def _():
        for e in range(N_EXPERTS):
            lax.fori_loop((pend_ref[e - 1] if e else 0) + cnt_ref[e], pend_ref[e], fill_pad, 0)
        lax.fori_loop(nact_ref[0] * MOE_BLOCK, MOE_ROWS, fill_pad, 0)

    t0 = i * tc

    def scatter(r, carry):
        for kk in range(TOP_K):
            slot_ref[dest_ref[r * TOP_K + kk]] = kk * TOKENS + t0 + r
        return carry

    lax.fori_loop(0, tc, scatter, 0, unroll=8)


def _slot_table(dest_flat, counts, pend, nact):
    tc = 4096
    return pl.pallas_call(
        _slot_kernel,
        grid_spec=pltpu.PrefetchScalarGridSpec(
            num_scalar_prefetch=3,
            grid=(TOKENS // tc,),
            in_specs=[pl.BlockSpec((tc * TOP_K,), lambda i, *_: (i,), memory_space=pltpu.SMEM)],
            out_specs=pl.BlockSpec(memory_space=pltpu.SMEM),
        ),
        out_shape=jax.ShapeDtypeStruct((MOE_ROWS,), jnp.int32),
        compiler_params=_params("arbitrary"),
        name="moe_slots",
    )(pend, counts, nact, dest_flat)


def _moe_kernel(layer, be_ref, nact_ref, pend_ref, slot_ref,
                h_ref, wg_hbm, wl_hbm, wd_hbm, bg_ref, bl_ref, bd_ref, o4_ref,
                xbuf, ybuf, wstage, wbf, gsem, ssem, wsem):
    j = pl.program_id(0)
    nact = nact_ref[0]
    par = j % 2
    rows = MOE_BLOCK * ROW_CHUNKS
    weights = (wg_hbm, wl_hbm, wd_hbm)

    def slab(buf, p):
        return buf.at[pl.ds(pl.multiple_of(p * rows, rows), rows)]

    def tile_at(buf, first):
        return buf.at[pl.ds(pl.multiple_of(first, ROW_CHUNKS), ROW_CHUNKS)]

    def gather_start(blk, p):
        def body(r, carry):
            tok = slot_ref[blk * MOE_BLOCK + r] & (TOKENS - 1)
            pltpu.make_async_copy(tile_at(h_ref, tok * ROW_CHUNKS),
                                  tile_at(xbuf, p * rows + r * ROW_CHUNKS), gsem.at[p]).start()
            return carry

        lax.fori_loop(0, MOE_BLOCK, body, 0, unroll=8)

    def gather_wait(p):
        pltpu.make_async_copy(h_ref.at[pl.ds(0, rows)], slab(xbuf, p), gsem.at[p]).wait()

    def scatter_start(blk, p):
        def body(r, carry):
            row = slot_ref[blk * MOE_BLOCK + r]
            pltpu.make_async_copy(tile_at(ybuf, p * rows + r * ROW_CHUNKS),
                                  tile_at(o4_ref, row * ROW_CHUNKS), ssem.at[p]).start()
            return carry

        lax.fori_loop(0, MOE_BLOCK, body, 0, unroll=8)

    def scatter_wait(p):
        pltpu.make_async_copy(slab(ybuf, p), o4_ref.at[pl.ds(0, rows)], ssem.at[p]).wait()

    def weights_start(e):
        for i, w in enumerate(weights):
            pltpu.make_async_copy(w.at[layer, e], wstage.at[i], wsem.at[i]).start()

    def weights_wait():
        for i, w in enumerate(weights):
            pltpu.make_async_copy(w.at[layer, 0], wstage.at[i], wsem.at[i]).wait()

    @pl.when(j == 0)
    def _():
        ybuf[...] = jnp.zeros_like(ybuf)
        dump = o4_ref.at[pl.ds(TOP_K * TOKENS * ROW_CHUNKS, 2 * rows)]
        cp = pltpu.make_async_copy(ybuf, dump, ssem.at[0])
        cp.start()
        cp.wait()
        gather_start(0, 0)
        weights_start(be_ref[0])

    @pl.when(j < nact)
    def _():
        e = be_ref[j]

        @pl.when((j == 0) | (e != be_ref[jnp.maximum(j - 1, 0)]))
        def _():
            weights_wait()
            for i in range(3):
                wbf[i] = wstage[i].astype(BF16)
            nxt = pend_ref[e] // MOE_BLOCK

            @pl.when(nxt < nact)
            def _():
                weights_start(be_ref[jnp.minimum(nxt, MOE_NBLOCKS - 1)])

        gather_wait(par)

        @pl.when(j >= 2)
        def _():
            scatter_wait(par)

        gather_start(jnp.minimum(j + 1, nact - 1), 1 - par)

        @pl.when(j >= 1)
        def _():
            scatter_start(j - 1, 1 - par)

        x = _load_row_tiles(xbuf, par * rows, MOE_BLOCK).astype(BF16)
        g = jnp.minimum(_dot(x, wbf[0]) + bg_ref[...], SWIGLU_LIMIT)
        lin = jnp.clip(_dot(x, wbf[1]) + bl_ref[...], -SWIGLU_LIMIT, SWIGLU_LIMIT)
        a = g * jax.nn.sigmoid(SWIGLU_ALPHA * g) * (lin + 1.0)
        _store_row_tiles(ybuf, par * rows, _dot(a.astype(BF16), wbf[2]) + bd_ref[...])

        @pl.when(j == nact - 1)
        def _():
            scatter_start(j, par)
            gather_wait(1 - par)

            @pl.when(j >= 1)
            def _():
                scatter_wait(1 - par)

            scatter_wait(par)


def _moe_experts(l, h_tiles, block_e, nact, pend, slot_row, w_glu, b_glu, w_lin, b_lin, w_down, b_down):
    rows = MOE_BLOCK * ROW_CHUNKS
    bsel = lambda j, be, na, pe, sl: (l, be[jnp.minimum(j, na[0] - 1)], 0, 0)
    bspec = pl.BlockSpec((None, None, 1, D_MODEL), bsel)
    b3 = lambda b: b.reshape(DEPTH, N_EXPERTS, 1, D_MODEL)
    hbm = pl.BlockSpec(memory_space=pl.ANY)
    return pl.pallas_call(
        functools.partial(_moe_kernel, l),
        grid_spec=pltpu.PrefetchScalarGridSpec(
            num_scalar_prefetch=4,
            grid=(MOE_NBLOCKS,),
            in_specs=[hbm, hbm, hbm, hbm, bspec, bspec, bspec],
            out_specs=hbm,
            scratch_shapes=[pltpu.VMEM((2 * rows, LANES), F32), pltpu.VMEM((2 * rows, LANES), F32),
                            pltpu.VMEM((3, D_MODEL, D_MODEL), F32),
                            pltpu.VMEM((3, D_MODEL, D_MODEL), BF16),
                            pltpu.SemaphoreType.DMA((2,)), pltpu.SemaphoreType.DMA((2,)),
                            pltpu.SemaphoreType.DMA((3,))],
        ),
        out_shape=jax.ShapeDtypeStruct(((TOP_K * TOKENS + 2 * MOE_BLOCK) * ROW_CHUNKS, LANES), F32),
        compiler_params=_params("arbitrary"),
        name="moe_experts",
    )(block_e, nact, pend, slot_row, h_tiles, w_glu, w_lin, w_down, b3(b_glu), b3(b_lin), b3(b_down))


def _combine_kernel(y0_ref, y1_ref, y2_ref, y3_ref, gt_ref, x_ref, gf_ref, lng_ref, lnb_ref, xo_ref):
    tm = x_ref.shape[0]
    gt = gt_ref[...]
    o = gt[:, 0:1] * _load_row_tiles(y0_ref, 0, tm)
    for kk, y_ref in enumerate((y1_ref, y2_ref, y3_ref), start=1):
        o = o + gt[:, kk:kk + 1] * _load_row_tiles(y_ref, 0, tm)
    xo_ref[...] = (_ln(DEEPNORM_ALPHA * x_ref[...] + gf_ref[...] * o, LN_EPS) * lng_ref[...]
                   + lnb_ref[...])


def _combine(o4, gates, x, modl, ln_g, ln_b):
    tm = ROW_TILE
    row = pl.BlockSpec((tm, D_MODEL), lambda i: (i, 0))
    vec = pl.BlockSpec((1, D_MODEL), lambda i: (0, 0))
    nt = TOKENS // tm
    yspec = lambda kk: pl.BlockSpec((tm * ROW_CHUNKS, LANES), lambda i: (kk * nt + i, 0))
    return pl.pallas_call(
        _combine_kernel,
        grid=(nt,),
        in_specs=[yspec(0), yspec(1), yspec(2), yspec(3),
                  pl.BlockSpec((tm, 8), lambda i: (i, 0)),
                  row, _mod_spec(5, tm), vec, vec],
        out_specs=row,
        out_shape=jax.ShapeDtypeStruct((TOKENS, D_MODEL), F32),
        compiler_params=_params("arbitrary"),
        name="moe_combine",
    )(o4, o4, o4, o4, gates, x, modl, ln_g.reshape(1, -1), ln_b.reshape(1, -1))


def _rope_table(positions):
    half = QK_ROPE // 2
    inv_freq = ROPE_THETA ** (-jnp.arange(0, QK_ROPE, 2, dtype=F32) / QK_ROPE)
    ang = positions.astype(F32)[..., None] * inv_freq
    cos, sin = jnp.cos(ang), jnp.sin(ang)
    z = lambda n: jnp.zeros(ang.shape[:-1] + (n,), F32)
    tail = HEAD_PAD - QK_NOPE - QK_ROPE
    c = jnp.concatenate([jnp.ones(ang.shape[:-1] + (QK_NOPE,), F32), cos, cos, z(tail)], -1)
    s1 = jnp.concatenate([z(QK_NOPE), -sin, z(half), z(tail)], -1)
    s2 = jnp.concatenate([z(QK_NOPE), z(half), sin, z(tail)], -1)
    return jnp.concatenate([c, s1, s2], -1).reshape(TOKENS, 3 * LANES)


def _moe_layer(l, x, h, top_e, gates, modl, ln_g, ln_b, w_glu, b_glu, w_lin, b_lin, w_down, b_down):
    dest_flat, block_e, counts, pend, nact = _rank(top_e)
    slot_row = _slot_table(dest_flat, counts, pend, nact)
    o4 = _moe_experts(l, h, block_e, nact, pend, slot_row, w_glu, b_glu, w_lin, b_lin, w_down, b_down)
    return _combine(o4, gates, x, modl, ln_g[l], ln_b[l])


def kernel(x, c, positions, ada_w, ada_b, ln_mix_g, ln_mix_b, ln_ffn_g, ln_ffn_b, mla_w_in, mla_q_norm_g, mla_kv_norm_g, mla_w_uq, mla_w_uk, mla_w_uv, mix_a_w_out, conv_w_in, conv_w, conv_w_out, moe_w_router, moe_b_router, moe_w_glu, moe_b_glu, moe_w_lin, moe_b_lin, moe_w_down, moe_b_down):
    mod = _modulation(c, ada_w, ada_b)
    rope_tab = _rope_table(positions)
    x = x.reshape(TOKENS, D_MODEL)
    for l in range(DEPTH):
        modl = mod[l]
        i = l // 2
        if l % 2 == 0:
            q, k, v, a_t, b_t = _pre0(x, modl, rope_tab, mla_w_in[i], mla_q_norm_g[i],
                                      mla_kv_norm_g[i], mla_w_uq[i], mla_w_uk[i], mla_w_uv[i])
            o_a = _attention(q, k, v)
            o_b = _fnet(a_t, b_t)
            x, h, top_e, gates = _post0(o_a, o_b, mix_a_w_out[i], x, modl, ln_mix_g[l], ln_mix_b[l],
                                        moe_w_router[l], moe_b_router[l])
        else:
            gate_b, u = _convin(x, modl, conv_w_in[i])
            x, h, top_e, gates = _post1(u, gate_b, conv_w[i], conv_w_out[i], x, modl, ln_mix_g[l],
                                        ln_mix_b[l], moe_w_router[l], moe_b_router[l])
        x = _moe_layer(l, x, h, top_e, gates, modl, ln_ffn_g, ln_ffn_b, moe_w_glu, moe_b_glu,
                       moe_w_lin, moe_b_lin, moe_w_down, moe_b_down)
    return x.reshape(BATCH, SEQ, D_MODEL)
```

```python
import functools
import math

import numpy as np
import jax
import jax.numpy as jnp
from jax import lax
from jax.experimental import pallas as pl
from jax.experimental.pallas import tpu as pltpu

D_MODEL = 1024
BATCH = 4
SEQ = 4096
DEPTH = 2
TOKENS = BATCH * SEQ

MLA_HEADS = 8
QK_NOPE = 64
QK_ROPE = 32
V_HEAD = 64
Q_RANK = 256
KV_RANK = 128
ROPE_THETA = 10000.0
MLA_OUT = MLA_HEADS * V_HEAD
FNET_GROUPS = 8
FNET_GROUP_DIM = 64
FNET_WIDTH = FNET_GROUPS * FNET_GROUP_DIM
CONV_WIDTH = 3
N_EXPERTS = 32
TOP_K = 4
SWIGLU_LIMIT = 7.0
SWIGLU_ALPHA = 1.702
DEEPNORM_ALPHA = (2 * DEPTH) ** 0.25
LN_EPS = 1e-5
MOD_EPS = 1e-6
RMS_EPS = 1e-6
QK_SCALE = (QK_NOPE + QK_ROPE) ** -0.5
LOG2E = math.log2(math.e)

LANES = 128
HEAD_PAD = 128
ROW_CHUNKS = D_MODEL // LANES
VMEM_LIMIT = 56 * 1024 * 1024

ROW_TILE = 256
ATT_Q_TILE = 256
FNET_M_TILE = 1024
FNET_K_TILE = 512
RANK_TILE = 512
MOE_BLOCK = 256
MOE_ROWS = TOKENS * TOP_K + N_EXPERTS * MOE_BLOCK
MOE_NBLOCKS = MOE_ROWS // MOE_BLOCK
NEG_BIG = -1e30

BF16 = jnp.bfloat16
F32 = jnp.float32


def _params(*sem):
    return pltpu.CompilerParams(dimension_semantics=sem, vmem_limit_bytes=VMEM_LIMIT)


def _ln(v, eps):
    mu = jnp.mean(v, axis=-1, keepdims=True)
    d = v - mu
    var = jnp.mean(d * d, axis=-1, keepdims=True)
    return d * lax.rsqrt(var + eps)


def _rms(v, g):
    return v * lax.rsqrt(jnp.mean(v * v, axis=-1, keepdims=True) + RMS_EPS) * g


def _dot(a, b):
    return jnp.dot(a, b, preferred_element_type=F32)


def _store_row_tiles(ref, base, val):
    rows = val.shape[0]
    for ch in range(ROW_CHUNKS):
        ref[pl.ds(base + ch, rows, stride=ROW_CHUNKS), :] = val[:, ch * LANES:(ch + 1) * LANES]


def _load_row_tiles(ref, base, rows):
    return jnp.concatenate(
        [ref[pl.ds(base + ch, rows, stride=ROW_CHUNKS), :] for ch in range(ROW_CHUNKS)], axis=1)


def _mod_kernel(c_ref, w_ref, b_ref, o_ref):
    c = c_ref[...]
    cond = c * jax.nn.sigmoid(c)
    o_ref[...] = _dot(cond, w_ref[...]) + b_ref[...]


def _modulation(c, ada_w, ada_b):
    tn = 1536
    n = 6 * D_MODEL
    c8 = jnp.zeros((8, D_MODEL), F32).at[:BATCH].set(c)
    out = pl.pallas_call(
        _mod_kernel,
        grid=(DEPTH, n // tn),
        in_specs=[
            pl.BlockSpec((8, D_MODEL), lambda l, j: (0, 0)),
            pl.BlockSpec((None, D_MODEL, tn), lambda l, j: (l, 0, j)),
            pl.BlockSpec((None, 1, tn), lambda l, j: (l, 0, j)),
        ],
        out_specs=pl.BlockSpec((None, 8, tn), lambda l, j: (l, 0, j)),
        out_shape=jax.ShapeDtypeStruct((DEPTH, 8, n), F32),
        compiler_params=_params("arbitrary", "arbitrary"),
        name="adaln_mod",
    )(c8, ada_w, ada_b.reshape(DEPTH, 1, n))
    return out[:, :BATCH].reshape(DEPTH, BATCH, 6, 1, D_MODEL)


def _mod_spec(chunk, tile):
    return pl.BlockSpec((None, None, 1, D_MODEL),
                        lambda i: ((i * tile) // SEQ, chunk, 0, 0))


def _rope(t, tab):
    c, s1, s2 = tab[:, :LANES], tab[:, LANES:2 * LANES], tab[:, 2 * LANES:]
    outs = []
    for h in range(MLA_HEADS):
        th = t[:, h * HEAD_PAD:(h + 1) * HEAD_PAD]
        outs.append(th * c + pltpu.roll(th, HEAD_PAD - QK_ROPE // 2, 1) * s1
                    + pltpu.roll(th, QK_ROPE // 2, 1) * s2)
    return jnp.concatenate(outs, axis=1)


def _pre0_kernel(x_ref, sc_ref, sh_ref, rope_ref, win_ref, gq_ref, gkv_ref, wuq_ref, wk_ref,
                 wuv_ref, dft_ref, q_ref, k_ref, v_ref, a_ref, b_ref):
    h = _ln(x_ref[...], MOD_EPS) * (1.0 + sc_ref[...]) + sh_ref[...]
    p = _dot(h.astype(BF16), win_ref[...])
    c_q = p[:, :Q_RANK]
    c_kv = p[:, Q_RANK:Q_RANK + KV_RANK]
    u_f = p[:, Q_RANK + KV_RANK:Q_RANK + KV_RANK + FNET_WIDTH]
    k_r = p[:, Q_RANK + KV_RANK + FNET_WIDTH:]
    tab = rope_ref[...]
    q = _dot(_rms(c_q, gq_ref[...]).astype(BF16), wuq_ref[...]) * (QK_SCALE * LOG2E)
    q_ref[...] = _rope(q, tab).astype(BF16)
    ckv = _rms(c_kv, gkv_ref[...]).astype(BF16)
    kin = jnp.concatenate([ckv, k_r.astype(BF16)], axis=1)
    k_ref[...] = _rope(_dot(kin, wk_ref[...]), tab).astype(BF16)
    v_ref[...] = _dot(ckv, wuv_ref[...]).astype(BF16)
    ab = _dot(u_f.astype(BF16), dft_ref[...])
    a_ref[...] = ab[:, :FNET_WIDTH].astype(BF16)
    b_ref[...] = ab[:, FNET_WIDTH:].astype(BF16)


def _channel_dft():
    n = np.arange(FNET_GROUP_DIM)
    ang = 2.0 * np.pi * ((n[:, None] * n[None, :]) % FNET_GROUP_DIM) / FNET_GROUP_DIM
    norm = 1.0 / math.sqrt(SEQ * FNET_GROUP_DIM)
    eye = np.eye(FNET_GROUPS)
    cc = np.kron(eye, np.cos(ang) * norm)
    ss = np.kron(eye, np.sin(ang) * norm)
    return jnp.asarray(np.concatenate([cc, ss], axis=1), BF16)


def _pre0(x, modl, rope_tab, w_in, gq, gkv, w_uq, w_uk, w_uv):
    tm = ROW_TILE
    nq = Q_RANK + KV_RANK
    w_in_r = jnp.concatenate(
        [w_in[:, :nq], w_in[:, nq + QK_ROPE:], w_in[:, nq:nq + QK_ROPE],
         jnp.zeros((D_MODEL, LANES - QK_ROPE), F32)], axis=1).astype(BF16)
    wuq_p = jnp.pad(w_uq.reshape(Q_RANK, MLA_HEADS, QK_NOPE + QK_ROPE),
                    ((0, 0), (0, 0), (0, HEAD_PAD - QK_NOPE - QK_ROPE)))
    wuq_p = wuq_p.reshape(Q_RANK, MLA_HEADS * HEAD_PAD).astype(BF16)
    wuk_p = jnp.pad(w_uk.reshape(KV_RANK, MLA_HEADS, QK_NOPE),
                    ((0, 0), (0, 0), (0, HEAD_PAD - QK_NOPE))).reshape(KV_RANK, MLA_HEADS * HEAD_PAD)
    place = np.zeros((LANES, MLA_HEADS, HEAD_PAD), np.float32)
    for j in range(QK_ROPE):
        place[j, :, QK_NOPE + j] = 1.0
    wk_p = jnp.concatenate([wuk_p, jnp.asarray(place.reshape(LANES, -1))], axis=0).astype(BF16)
    full = lambda shape: pl.BlockSpec(shape, lambda i: (0,) * len(shape))
    nst = SEQ // tm
    wide = MLA_HEADS * HEAD_PAD
    return pl.pallas_call(
        _pre0_kernel,
        grid=(TOKENS // tm,),
        in_specs=[
            pl.BlockSpec((tm, D_MODEL), lambda i: (i, 0)),
            _mod_spec(1, tm), _mod_spec(0, tm),
            pl.BlockSpec((tm, 3 * LANES), lambda i: (i, 0)),
            full((D_MODEL, D_MODEL)), full((1, Q_RANK)), full((1, KV_RANK)),
            full((Q_RANK, wide)), full((2 * LANES, wide)), full((KV_RANK, MLA_OUT)),
            full((FNET_WIDTH, 2 * FNET_WIDTH)),
        ],
        out_specs=[
            pl.BlockSpec((tm, wide), lambda i: (i, 0)),
            pl.BlockSpec((tm, wide), lambda i: (i, 0)),
            pl.BlockSpec((tm, MLA_OUT), lambda i: (i, 0)),
            pl.BlockSpec((tm, FNET_WIDTH), lambda i: (i % nst, i // nst)),
            pl.BlockSpec((tm, FNET_WIDTH), lambda i: (i % nst, i // nst)),
        ],
        out_shape=[
            jax.ShapeDtypeStruct((TOKENS, wide), BF16),
            jax.ShapeDtypeStruct((TOKENS, wide), BF16),
            jax.ShapeDtypeStruct((TOKENS, MLA_OUT), BF16),
            jax.ShapeDtypeStruct((SEQ, BATCH * FNET_WIDTH), BF16),
            jax.ShapeDtypeStruct((SEQ, BATCH * FNET_WIDTH), BF16),
        ],
        compiler_params=_params("arbitrary"),
        name="mla_fnet_front",
    )(x, modl, modl, rope_tab, w_in_r, gq.reshape(1, -1), gkv.reshape(1, -1), wuq_p, wk_p,
      w_uv.astype(BF16), _channel_dft())


def _attn_kernel(q_ref, k_ref, v_ref, o_ref):
    outs = []
    for hh in range(2):
        q = q_ref[:, hh * HEAD_PAD:(hh + 1) * HEAD_PAD]
        k = k_ref[:, hh * HEAD_PAD:(hh + 1) * HEAD_PAD]
        s = lax.dot_general(q, k, (((1,), (1,)), ((), ())), preferred_element_type=F32)
        m = jnp.max(s, axis=-1, keepdims=True)
        p = jnp.exp2(s - m)
        l = jnp.sum(p, axis=-1, keepdims=True)
        outs.append(_dot(p.astype(BF16), v_ref[...]) / l)
    lane = lax.broadcasted_iota(jnp.int32, outs[0].shape, 1)
    o_ref[...] = jnp.where(lane < V_HEAD, outs[0], outs[1]).astype(BF16)


def _attention(q, k, v):
    tq = ATT_Q_TILE
    wide = MLA_HEADS * HEAD_PAD
    q = q.reshape(BATCH, SEQ, wide)
    k = k.reshape(BATCH, SEQ, wide)
    v = v.reshape(BATCH, SEQ, MLA_OUT)
    out = pl.pallas_call(
        _attn_kernel,
        grid=(BATCH, MLA_HEADS // 2, SEQ // tq),
        in_specs=[
            pl.BlockSpec((None, tq, 2 * HEAD_PAD), lambda b, h, i: (b, i, h)),
            pl.BlockSpec((None, SEQ, 2 * HEAD_PAD), lambda b, h, i: (b, 0, h)),
            pl.BlockSpec((None, SEQ, 2 * V_HEAD), lambda b, h, i: (b, 0, h)),
        ],
        out_specs=pl.BlockSpec((None, tq, 2 * V_HEAD), lambda b, h, i: (b, i, h)),
        out_shape=jax.ShapeDtypeStruct((BATCH, SEQ, MLA_OUT), BF16),
        compiler_params=_params("arbitrary", "arbitrary", "arbitrary"),
        name="mla_attention",
    )(q, k, v)
    return out.reshape(TOKENS, MLA_OUT)


def _fnet_kernel(c_ref, s_ref, a_ref, b_ref, o_ref, acc_ref):
    kk = pl.program_id(1)

    @pl.when(kk == 0)
    def _():
        acc_ref[...] = jnp.zeros_like(acc_ref)

    acc_ref[...] += _dot(c_ref[...], a_ref[...]) + _dot(s_ref[...], b_ref[...])

    @pl.when(kk == pl.num_programs(1) - 1)
    def _():
        o_ref[...] = acc_ref[...].astype(BF16)


def _sequence_dft():
    r = int(math.isqrt(SEQ))
    k = jnp.arange(SEQ, dtype=jnp.int32)[:, None]
    j = jnp.arange(r, dtype=jnp.int32)[None, :]
    alpha = (2.0 * np.pi / r) * ((k * j) % r).astype(F32)
    beta = (2.0 * np.pi / SEQ) * ((k * j) % SEQ).astype(F32)
    ca, sa, cb, sb = jnp.cos(alpha), jnp.sin(alpha), jnp.cos(beta), jnp.sin(beta)
    cs = ca[:, :, None] * cb[:, None, :] - sa[:, :, None] * sb[:, None, :]
    sn = -(sa[:, :, None] * cb[:, None, :] + ca[:, :, None] * sb[:, None, :])
    return cs.reshape(SEQ, SEQ).astype(BF16), sn.reshape(SEQ, SEQ).astype(BF16)


def _fnet(a_t, b_t):
    tm, tk = FNET_M_TILE, FNET_K_TILE
    n = BATCH * FNET_WIDTH
    cs, sn = _sequence_dft()
    return pl.pallas_call(
        _fnet_kernel,
        grid=(SEQ // tm, SEQ // tk),
        in_specs=[
            pl.BlockSpec((tm, tk), lambda i, kk: (i, kk)),
            pl.BlockSpec((tm, tk), lambda i, kk: (i, kk)),
            pl.BlockSpec((tk, n), lambda i, kk: (kk, 0)),
            pl.BlockSpec((tk, n), lambda i, kk: (kk, 0)),
        ],
        out_specs=pl.BlockSpec((tm, n), lambda i, kk: (i, 0)),
        out_shape=jax.ShapeDtypeStruct((SEQ, n), BF16),
        scratch_shapes=[pltpu.VMEM((tm, n), F32)],
        compiler_params=_params("arbitrary", "arbitrary"),
        name="fnet_seq_dft",
    )(cs, sn, a_t, b_t)


def _top4(logits):
    rows = logits.shape[0]
    lane = lax.broadcasted_iota(jnp.int32, (rows, LANES), 1).astype(F32)
    work = logits
    vals, idxs = [], []
    for _ in range(TOP_K):
        m = jnp.max(work, axis=-1, keepdims=True)
        idx = jnp.min(jnp.where(work == m, lane, float(LANES)), axis=-1, keepdims=True)
        vals.append(m)
        idxs.append(idx.astype(jnp.int32))
        work = jnp.where(lane == idx, -jnp.inf, work)
    es = [jnp.exp(v - vals[0]) for v in vals]
    den = es[0] + es[1] + es[2] + es[3]
    lane8 = lax.broadcasted_iota(jnp.int32, (rows, 8), 1)
    top_e = jnp.zeros((rows, 8), jnp.int32)
    gates = jnp.zeros((rows, 8), F32)
    for kk in range(TOP_K):
        top_e = jnp.where(lane8 == kk, idxs[kk], top_e)
        gates = jnp.where(lane8 == kk, es[kk] / den, gates)
    return top_e, gates


def _mixer_tail(o, x_ref, gm_ref, scf_ref, shf_ref, lng_ref, lnb_ref, wr_ref, br_ref,
                xo_ref, h_ref, te_ref, gt_ref):
    xn = _ln(DEEPNORM_ALPHA * x_ref[...] + gm_ref[...] * o, LN_EPS) * lng_ref[...] + lnb_ref[...]
    xo_ref[...] = xn
    h = _ln(xn, MOD_EPS) * (1.0 + scf_ref[...]) + shf_ref[...]
    _store_row_tiles(h_ref, 0, h)
    logits = _dot(h, wr_ref[...]) + br_ref[...]
    top_e, gates = _top4(logits)
    te_ref[...] = top_e
    gt_ref[...] = gates


def _post0_kernel(oa_ref, ob_ref, wa_ref, wb_ref, *rest):
    o = _dot(oa_ref[...], wa_ref[...]) + _dot(ob_ref[...], wb_ref[...])
    _mixer_tail(o, *rest)


def _post1_kernel(u_ref, up_ref, un_ref, gb_ref, cw_ref, wo_ref, *rest):
    tm = u_ref.shape[0]
    i = pl.program_id(0)
    s0 = (i * tm) % SEQ
    u = u_ref[...].astype(F32)
    prev_row = jnp.where(s0 > 0, up_ref[...].astype(F32)[15:16, :], 0.0)
    next_row = jnp.where(s0 + tm < SEQ, un_ref[...].astype(F32)[0:1, :], 0.0)
    row = lax.broadcasted_iota(jnp.int32, u.shape, 0)
    u_m1 = jnp.where(row == 0, prev_row, pltpu.roll(u, 1, 0))
    u_p1 = jnp.where(row == tm - 1, next_row, pltpu.roll(u, tm - 1, 0))
    cw = cw_ref[...]
    y = u_m1 * cw[0:1, :] + u * cw[1:2, :] + u_p1 * cw[2:3, :]
    g = gb_ref[...].astype(F32) * y
    o = _dot(g.astype(BF16), wo_ref[...])
    _mixer_tail(o, *rest)


def _tail_specs(tm):
    row = lambda w: pl.BlockSpec((tm, w), lambda i: (i, 0))
    full = lambda shape: pl.BlockSpec(shape, lambda i: (0,) * len(shape))
    in_specs = [row(D_MODEL), _mod_spec(2, tm), _mod_spec(4, tm), _mod_spec(3, tm),
                full((1, D_MODEL)), full((1, D_MODEL)), full((D_MODEL, LANES)), full((1, LANES))]
    out_specs = [row(D_MODEL), pl.BlockSpec((tm * ROW_CHUNKS, LANES), lambda i: (i, 0)),
                 row(8), row(8)]
    out_shape = [jax.ShapeDtypeStruct((TOKENS, D_MODEL), F32),
                 jax.ShapeDtypeStruct((TOKENS * ROW_CHUNKS, LANES), F32),
                 jax.ShapeDtypeStruct((TOKENS, 8), jnp.int32),
                 jax.ShapeDtypeStruct((TOKENS, 8), F32)]
    return in_specs, out_specs, out_shape


def _tail_args(x, modl, ln_g, ln_b, w_router, b_router):
    wr = jnp.pad(w_router, ((0, 0), (0, LANES - N_EXPERTS)))
    br = jnp.concatenate([b_router, jnp.full((LANES - N_EXPERTS,), NEG_BIG, F32)]).reshape(1, LANES)
    return (x, modl, modl, modl, ln_g.reshape(1, -1), ln_b.reshape(1, -1), wr, br)


def _post0(o_a, o_b, w_out, x, modl, ln_g, ln_b, w_router, b_router):
    tm = ROW_TILE
    nst = SEQ // tm
    tin, tout, tshape = _tail_specs(tm)
    w = w_out.astype(BF16)
    full = lambda shape: pl.BlockSpec(shape, lambda i: (0,) * len(shape))
    return pl.pallas_call(
        _post0_kernel,
        grid=(TOKENS // tm,),
        in_specs=[pl.BlockSpec((tm, MLA_OUT), lambda i: (i, 0)),
                  pl.BlockSpec((tm, FNET_WIDTH), lambda i: (i % nst, i // nst)),
                  full((MLA_OUT, D_MODEL)), full((FNET_WIDTH, D_MODEL))] + tin,
        out_specs=tout,
        out_shape=tshape,
        compiler_params=_params("arbitrary"),
        name="mix_a_tail",
    )(o_a, o_b, w[:MLA_OUT], w[MLA_OUT:], *_tail_args(x, modl, ln_g, ln_b, w_router, b_router))


def _convin_kernel(x_ref, sc_ref, sh_ref, w_ref, gb_ref, u_ref):
    h = _ln(x_ref[...], MOD_EPS) * (1.0 + sc_ref[...]) + sh_ref[...]
    p = _dot(h.astype(BF16), w_ref[...])
    gb_ref[...] = p[:, :D_MODEL].astype(BF16)
    u_ref[...] = (p[:, D_MODEL:2 * D_MODEL] * p[:, 2 * D_MODEL:]).astype(BF16)


def _convin(x, modl, w_in):
    tm = ROW_TILE
    row = pl.BlockSpec((tm, D_MODEL), lambda i: (i, 0))
    return pl.pallas_call(
        _convin_kernel,
        grid=(TOKENS // tm,),
        in_specs=[row, _mod_spec(1, tm), _mod_spec(0, tm),
                  pl.BlockSpec((D_MODEL, 3 * D_MODEL), lambda i: (0, 0))],
        out_specs=[row, row],
        out_shape=[jax.ShapeDtypeStruct((TOKENS, D_MODEL), BF16)] * 2,
        compiler_params=_params("arbitrary"),
        name="conv_front",
    )(x, modl, modl, w_in.astype(BF16))


def _post1(u, gate_b, conv_w, w_out, x, modl, ln_g, ln_b, w_router, b_router):
    tm = ROW_TILE
    halo = 16
    tin, tout, tshape = _tail_specs(tm)
    nh = TOKENS // halo
    cw = jnp.zeros((8, D_MODEL), F32).at[:CONV_WIDTH].set(conv_w)
    row = pl.BlockSpec((tm, D_MODEL), lambda i: (i, 0))
    return pl.pallas_call(
        _post1_kernel,
        grid=(TOKENS // tm,),
        in_specs=[row,
                  pl.BlockSpec((halo, D_MODEL), lambda i: (jnp.maximum(i * (tm // halo) - 1, 0), 0)),
                  pl.BlockSpec((halo, D_MODEL),
                               lambda i: (jnp.minimum((i + 1) * (tm // halo), nh - 1), 0)),
                  row,
                  pl.BlockSpec((8, D_MODEL), lambda i: (0, 0)),
                  pl.BlockSpec((D_MODEL, D_MODEL), lambda i: (0, 0))] + tin,
        out_specs=tout,
        out_shape=tshape,
        compiler_params=_params("arbitrary"),
        name="conv_tail",
    )(u, u, u, gate_b, cw, w_out.astype(BF16),
      *_tail_args(x, modl, ln_g, ln_b, w_router, b_router))


def _lane_cumsum(v):
    lane = lax.broadcasted_iota(jnp.int32, v.shape, 1)
    s = 1
    while s < N_EXPERTS:
        v = v + jnp.where(lane >= s, pltpu.roll(v, s, 1), 0.0)
        s *= 2
    return v


def _rank_kernel(te_ref, dest_ref, meta_ref, cnt_ref, carry_ref, pstart_ref):
    ph = pl.program_id(0)
    i = pl.program_id(1)
    tb = te_ref.shape[0]
    lane = lax.broadcasted_iota(jnp.int32, (tb, LANES), 1)
    te = te_ref[...]
    onehot = [lane == te[:, kk:kk + 1] for kk in range(TOP_K)]
    msum = sum(oh.astype(F32) for oh in onehot)
    colsum = jnp.sum(msum, axis=0, keepdims=True)

    @pl.when((ph == 0) & (i == 0))
    def _():
        cnt_ref[...] = jnp.zeros_like(cnt_ref)

    @pl.when(ph == 0)
    def _():
        cnt_ref[...] += jnp.broadcast_to(colsum, cnt_ref.shape)

    @pl.when((ph == 1) & (i == 0))
    def _():
        cnt = cnt_ref[...]
        padded = jnp.floor((cnt + (MOE_BLOCK - 1)) * (1.0 / MOE_BLOCK)) * MOE_BLOCK
        pend = _lane_cumsum(padded)
        pstart_ref[...] = pend - padded
        carry_ref[...] = jnp.zeros_like(carry_ref)
        lane8 = lax.broadcasted_iota(jnp.int32, (8, LANES), 1)
        row8 = lax.broadcasted_iota(jnp.int32, (8, LANES), 0)
        thr = ((row8 * LANES + lane8) * MOE_BLOCK).astype(F32)
        blk = jnp.zeros((8, LANES), F32)
        for e in range(N_EXPERTS):
            pe = jnp.sum(jnp.where(lane8 == e, pend, 0.0), axis=1, keepdims=True)
            blk = blk + (pe <= thr).astype(F32)
        blk = jnp.minimum(blk, N_EXPERTS - 1.0)
        total = jnp.sum(jnp.where(lane8 == N_EXPERTS - 1, pend, 0.0), axis=1, keepdims=True)
        nact = jnp.broadcast_to(total * (1.0 / MOE_BLOCK), (8, LANES))
        info = jnp.where(row8 == 0, cnt, jnp.where(row8 == 1, pend, nact))
        meta_ref[0:8, :] = blk.astype(jnp.int32)
        meta_ref[8:16, :] = info.astype(jnp.int32)

    @pl.when(ph == 1)
    def _():
        r = lax.broadcasted_iota(jnp.int32, (tb, tb), 0)
        c = lax.broadcasted_iota(jnp.int32, (tb, tb), 1)
        lower = (c < r).astype(BF16)
        prefix = _dot(lower, msum.astype(BF16))
        base = prefix + carry_ref[0:1, :] + pstart_ref[0:1, :]
        lane8 = lax.broadcasted_iota(jnp.int32, (tb, 8), 1)
        dest = jnp.zeros((tb, 8), jnp.int32)
        for kk in range(TOP_K):
            dk = jnp.sum(jnp.where(onehot[kk], base, 0.0), axis=1, keepdims=True)
            dest = jnp.where(lane8 == kk, dk.astype(jnp.int32), dest)
        dest_ref[...] = dest
        carry_ref[...] += jnp.broadcast_to(colsum, carry_ref.shape)


def _rank(top_e):
    tb = RANK_TILE
    dest, meta = pl.pallas_call(
        _rank_kernel,
        grid=(2, TOKENS // tb),
        in_specs=[pl.BlockSpec((tb, 8), lambda ph, i: (i, 0))],
        out_specs=[pl.BlockSpec((tb, 8), lambda ph, i: (i * ph, 0)),
                   pl.BlockSpec((16, LANES), lambda ph, i: (0, 0))],
        out_shape=[jax.ShapeDtypeStruct((TOKENS, 8), jnp.int32),
                   jax.ShapeDtypeStruct((16, LANES), jnp.int32)],
        scratch_shapes=[pltpu.VMEM((8, LANES), F32)] * 3,
        compiler_params=_params("arbitrary", "arbitrary"),
        name="moe_rank",
    )(top_e)
    dest_flat = dest[:, :TOP_K].reshape(-1)
    block_e = meta[0:8].reshape(-1)[:MOE_NBLOCKS]
    counts = meta[8, :N_EXPERTS]
    pend = meta[9, :N_EXPERTS]
    nact = meta[10, 0:1]
    return dest_flat, block_e, counts, pend, nact


def _slot_kernel(pend_ref, cnt_ref, nact_ref, dest_ref, slot_ref):
    i = pl.program_id(0)
    tc = dest_ref.shape[0] // TOP_K

    def fill_pad(s, carry):
        slot_ref[s + MOE_BLOCK] = TOP_K * TOKENS + (s & (2 * MOE_BLOCK - 1))
        return carry

    @pl.when(i == 0)
    def _():
        lax.fori_loop(-MOE_BLOCK, 0, fill_pad, 0)
        for e in range(N_EXPERTS):
            lax.fori_loop((pend_ref[e - 1] if e else 0) + cnt_ref[e], pend_ref[e], fill_pad, 0)
        lax.fori_loop(nact_ref[0] * MOE_BLOCK, MOE_ROWS, fill_pad, 0)

    t0 = i * tc

    def scatter(r, carry):
        for kk in range(TOP_K):
            slot_ref[dest_ref[r * TOP_K + kk] + MOE_BLOCK] = kk * TOKENS + t0 + r
        return carry

    lax.fori_loop(0, tc, scatter, 0, unroll=8)


def _slot_table(dest_flat, counts, pend, nact):
    tc = 4096
    return pl.pallas_call(
        _slot_kernel,
        grid_spec=pltpu.PrefetchScalarGridSpec(
            num_scalar_prefetch=3,
            grid=(TOKENS // tc,),
            in_specs=[pl.BlockSpec((tc * TOP_K,), lambda i, *_: (i,), memory_space=pltpu.SMEM)],
            out_specs=pl.BlockSpec(memory_space=pltpu.SMEM),
        ),
        out_shape=jax.ShapeDtypeStruct((MOE_ROWS + MOE_BLOCK,), jnp.int32),
        compiler_params=_params("arbitrary"),
        name="moe_slots",
    )(pend, counts, nact, dest_flat)


def _moe_kernel(layer, be_ref, nact_ref, pend_ref, slot_ref,
                h_ref, wg_hbm, wl_hbm, wd_hbm, bg0, bl0, bd0, bg1, bl1, bd1, o4_ref,
                rbuf, wstage, wbf, gsem, ssem, wsem):
    step = pl.program_id(0)
    nact = nact_ref[0]
    rows = MOE_BLOCK * ROW_CHUNKS
    weights = (wg_hbm, wl_hbm, wd_hbm)
    zero = be_ref[0] >> 16
    xoff = (0, rows)
    yoff = (2 * rows, 3 * rows)
    aoff = 4 * rows + zero

    def region(off):
        return rbuf.at[pl.ds(pl.multiple_of(off, ROW_CHUNKS), rows)]

    def tile_at(buf, first):
        return buf.at[pl.ds(pl.multiple_of(first, ROW_CHUNKS), ROW_CHUNKS)]

    def gather_row(blk, r, q):
        tok = slot_ref[(blk + 1) * MOE_BLOCK + r] & (TOKENS - 1)
        return pltpu.make_async_copy(tile_at(h_ref, tok * ROW_CHUNKS),
                                     tile_at(rbuf, xoff[q] + r * ROW_CHUNKS), gsem.at[q])

    def scatter_row(blk, r, q):
        row = slot_ref[(blk + 1) * MOE_BLOCK + r]
        return pltpu.make_async_copy(tile_at(rbuf, yoff[q] + r * ROW_CHUNKS),
                                     tile_at(o4_ref, row * ROW_CHUNKS), ssem.at[q])

    def gather_wait(q):
        pltpu.make_async_copy(h_ref.at[pl.ds(0, rows)], region(xoff[q]), gsem.at[q]).wait()

    def scatter_wait(q):
        pltpu.make_async_copy(region(yoff[q]), o4_ref.at[pl.ds(0, rows)], ssem.at[q]).wait()

    def weights_start(e):
        for i, w in enumerate(weights):
            pltpu.make_async_copy(w.at[layer, e], wstage.at[i], wsem.at[i]).start()

    def weights_wait():
        for i, w in enumerate(weights):
            pltpu.make_async_copy(w.at[layer, 0], wstage.at[i], wsem.at[i]).wait()

    @pl.when(step == 0)
    def _():
        for q in range(2):
            rbuf[pl.ds((2 + q) * rows, rows), :] = jnp.zeros((rows, LANES), F32)
            dump = o4_ref.at[pl.ds(TOP_K * TOKENS * ROW_CHUNKS + q * rows, rows)]
            cp = pltpu.make_async_copy(region(yoff[q]), dump, ssem.at[q])
            cp.start()
            cp.wait()

        def first_rows(r, carry):
            gather_row(0, r, 0).start()
            return carry

        lax.fori_loop(0, MOE_BLOCK, first_rows, 0, unroll=8)
        weights_start(be_ref[0])

    def run_block(b, q, bg_ref, bl_ref, bd_ref):
        o = 1 - q

        @pl.when(b < nact)
        def _():
            e = be_ref[b]

            @pl.when((b == 0) | (e != be_ref[jnp.maximum(b - 1, 0)]))
            def _():
                weights_wait()
                for i in range(3):
                    wbf[i] = wstage[i].astype(BF16)
                nxt = pend_ref[e] // MOE_BLOCK

                @pl.when(nxt < nact)
                def _():
                    weights_start(be_ref[jnp.minimum(nxt, MOE_NBLOCKS - 1)])

            gather_wait(q)

            @pl.when(b >= 1)
            def _():
                scatter_wait(q)

            x = _load_row_tiles(rbuf, xoff[q] + zero, MOE_BLOCK).astype(BF16)
            nb = jnp.minimum(b + 1, nact - 1)
            for r in range(MOE_BLOCK):
                gather_row(nb, r, o).start()
            for r in range(MOE_BLOCK):
                scatter_row(b - 1, r, o).start()

            g = jnp.minimum(_dot(x, wbf[0]) + bg_ref[...], SWIGLU_LIMIT)
            lin = jnp.clip(_dot(x, wbf[1]) + bl_ref[...], -SWIGLU_LIMIT, SWIGLU_LIMIT)
            a = g * jax.nn.sigmoid(SWIGLU_ALPHA * g) * (lin + 1.0)
            for ch in range(ROW_CHUNKS):
                rbuf[pl.ds(aoff + ch * MOE_BLOCK, MOE_BLOCK), :] = a[:, ch * LANES:(ch + 1) * LANES]
            a = jnp.concatenate([rbuf[pl.ds(aoff + ch * MOE_BLOCK, MOE_BLOCK), :]
                                 for ch in range(ROW_CHUNKS)], axis=1)
            _store_row_tiles(rbuf, yoff[q] + zero, _dot(a.astype(BF16), wbf[2]) + bd_ref[...])

            @pl.when(b == nact - 1)
            def _():
                def last_rows(r, carry):
                    scatter_row(b, r, q).start()
                    return carry

                lax.fori_loop(0, MOE_BLOCK, last_rows, 0, unroll=8)
                gather_wait(o)
                scatter_wait(o)
                scatter_wait(q)

    run_block(2 * step, 0, bg0, bl0, bd0)
    run_block(2 * step + 1, 1, bg1, bl1, bd1)


def _moe_experts(l, h_tiles, block_e, nact, pend, slot_row, w_glu, b_glu, w_lin, b_lin, w_down, b_down):
    rows = MOE_BLOCK * ROW_CHUNKS

    def bspec(q):
        return pl.BlockSpec((None, None, 1, D_MODEL),
                            lambda s, be, na, pe, sl: (l, be[jnp.minimum(2 * s + q, na[0] - 1)], 0, 0))

    b3 = lambda b: b.reshape(DEPTH, N_EXPERTS, 1, D_MODEL)
    hbm = pl.BlockSpec(memory_space=pl.ANY)
    biases = (b3(b_glu), b3(b_lin), b3(b_down))
    return pl.pallas_call(
        functools.partial(_moe_kernel, l),
        grid_spec=pltpu.PrefetchScalarGridSpec(
            num_scalar_prefetch=4,
            grid=(MOE_NBLOCKS // 2,),
            in_specs=[hbm, hbm, hbm, hbm] + [bspec(0)] * 3 + [bspec(1)] * 3,
            out_specs=hbm,
            scratch_shapes=[
                pltpu.VMEM((5 * rows, LANES), F32),
                pltpu.VMEM((3, D_MODEL, D_MODEL), F32),
                pltpu.VMEM((3, D_MODEL, D_MODEL), BF16),
                pltpu.SemaphoreType.DMA((2,)), pltpu.SemaphoreType.DMA((2,)),
                pltpu.SemaphoreType.DMA((3,))],
        ),
        out_shape=jax.ShapeDtypeStruct(((TOP_K * TOKENS + 2 * MOE_BLOCK) * ROW_CHUNKS, LANES), F32),
        compiler_params=_params("arbitrary"),
        name="moe_experts",
    )(block_e, nact, pend, slot_row, h_tiles, w_glu, w_lin, w_down, *biases, *biases)


def _combine_kernel(y0_ref, y1_ref, y2_ref, y3_ref, gt_ref, x_ref, gf_ref, lng_ref, lnb_ref, xo_ref):
    tm = x_ref.shape[0]
    gt = gt_ref[...]
    o = gt[:, 0:1] * _load_row_tiles(y0_ref, 0, tm)
    for kk, y_ref in enumerate((y1_ref, y2_ref, y3_ref), start=1):
        o = o + gt[:, kk:kk + 1] * _load_row_tiles(y_ref, 0, tm)
    xo_ref[...] = (_ln(DEEPNORM_ALPHA * x_ref[...] + gf_ref[...] * o, LN_EPS) * lng_ref[...]
                   + lnb_ref[...])


def _combine(o4, gates, x, modl, ln_g, ln_b):
    tm = ROW_TILE
    row = pl.BlockSpec((tm, D_MODEL), lambda i: (i, 0))
    vec = pl.BlockSpec((1, D_MODEL), lambda i: (0, 0))
    nt = TOKENS // tm
    yspec = lambda kk: pl.BlockSpec((tm * ROW_CHUNKS, LANES), lambda i: (kk * nt + i, 0))
    return pl.pallas_call(
        _combine_kernel,
        grid=(nt,),
        in_specs=[yspec(0), yspec(1), yspec(2), yspec(3),
                  pl.BlockSpec((tm, 8), lambda i: (i, 0)),
                  row, _mod_spec(5, tm), vec, vec],
        out_specs=row,
        out_shape=jax.ShapeDtypeStruct((TOKENS, D_MODEL), F32),
        compiler_params=_params("arbitrary"),
        name="moe_combine",
    )(o4, o4, o4, o4, gates, x, modl, ln_g.reshape(1, -1), ln_b.reshape(1, -1))


def _rope_table(positions):
    half = QK_ROPE // 2
    inv_freq = ROPE_THETA ** (-jnp.arange(0, QK_ROPE, 2, dtype=F32) / QK_ROPE)
    ang = positions.astype(F32)[..., None] * inv_freq
    cos, sin = jnp.cos(ang), jnp.sin(ang)
    z = lambda n: jnp.zeros(ang.shape[:-1] + (n,), F32)
    tail = HEAD_PAD - QK_NOPE - QK_ROPE
    c = jnp.concatenate([jnp.ones(ang.shape[:-1] + (QK_NOPE,), F32), cos, cos, z(tail)], -1)
    s1 = jnp.concatenate([z(QK_NOPE), -sin, z(half), z(tail)], -1)
    s2 = jnp.concatenate([z(QK_NOPE), z(half), sin, z(tail)], -1)
    return jnp.concatenate([c, s1, s2], -1).reshape(TOKENS, 3 * LANES)


def _moe_layer(l, x, h, top_e, gates, modl, ln_g, ln_b, w_glu, b_glu, w_lin, b_lin, w_down, b_down):
    dest_flat, block_e, counts, pend, nact = _rank(top_e)
    slot_row = _slot_table(dest_flat, counts, pend, nact)
    o4 = _moe_experts(l, h, block_e, nact, pend, slot_row, w_glu, b_glu, w_lin, b_lin, w_down, b_down)
    return _combine(o4, gates, x, modl, ln_g[l], ln_b[l])


def kernel(x, c, positions, ada_w, ada_b, ln_mix_g, ln_mix_b, ln_ffn_g, ln_ffn_b, mla_w_in, mla_q_norm_g, mla_kv_norm_g, mla_w_uq, mla_w_uk, mla_w_uv, mix_a_w_out, conv_w_in, conv_w, conv_w_out, moe_w_router, moe_b_router, moe_w_glu, moe_b_glu, moe_w_lin, moe_b_lin, moe_w_down, moe_b_down):
    mod = _modulation(c, ada_w, ada_b)
    rope_tab = _rope_table(positions)
    x = x.reshape(TOKENS, D_MODEL)
    for l in range(DEPTH):
        modl = mod[l]
        i = l // 2
        if l % 2 == 0:
            q, k, v, a_t, b_t = _pre0(x, modl, rope_tab, mla_w_in[i], mla_q_norm_g[i],
                                      mla_kv_norm_g[i], mla_w_uq[i], mla_w_uk[i], mla_w_uv[i])
            o_a = _attention(q, k, v)
            o_b = _fnet(a_t, b_t)
            x, h, top_e, gates = _post0(o_a, o_b, mix_a_w_out[i], x, modl, ln_mix_g[l], ln_mix_b[l],
                                        moe_w_router[l], moe_b_router[l])
        else:
            gate_b, u = _convin(x, modl, conv_w_in[i])
            x, h, top_e, gates = _post1(u, gate_b, conv_w[i], conv_w_out[i], x, modl, ln_mix_g[l],
                                        ln_mix_b[l], moe_w_router[l], moe_b_router[l])
        x = _moe_layer(l, x, h, top_e, gates, modl, ln_ffn_g, ln_ffn_b, moe_w_glu, moe_b_glu,
                       moe_w_lin, moe_b_lin, moe_w_down, moe_b_down)
    return x.reshape(BATCH, SEQ, D_MODEL)
```

```python
import functools
import math

import numpy as np
import jax
import jax.numpy as jnp
from jax import lax
from jax.experimental import pallas as pl
from jax.experimental.pallas import tpu as pltpu

D_MODEL = 1024
BATCH = 4
SEQ = 4096
DEPTH = 2
TOKENS = BATCH * SEQ

MLA_HEADS = 8
QK_NOPE = 64
QK_ROPE = 32
V_HEAD = 64
Q_RANK = 256
KV_RANK = 128
ROPE_THETA = 10000.0
MLA_OUT = MLA_HEADS * V_HEAD
FNET_GROUPS = 8
FNET_GROUP_DIM = 64
FNET_WIDTH = FNET_GROUPS * FNET_GROUP_DIM
CONV_WIDTH = 3
N_EXPERTS = 32
TOP_K = 4
SWIGLU_LIMIT = 7.0
SWIGLU_ALPHA = 1.702
DEEPNORM_ALPHA = (2 * DEPTH) ** 0.25
LN_EPS = 1e-5
MOD_EPS = 1e-6
RMS_EPS = 1e-6
QK_SCALE = (QK_NOPE + QK_ROPE) ** -0.5
LOG2E = math.log2(math.e)

LANES = 128
HEAD_PAD = 128
ROW_CHUNKS = D_MODEL // LANES
VMEM_LIMIT = 56 * 1024 * 1024

ROW_TILE = 256
ATT_Q_TILE = 256
FNET_M_TILE = 1024
FNET_K_TILE = 512
RANK_TILE = 1024
MOE_BLOCK = 256
MOE_ROWS = TOKENS * TOP_K + N_EXPERTS * MOE_BLOCK
MOE_NBLOCKS = MOE_ROWS // MOE_BLOCK
NEG_BIG = -1e30

BF16 = jnp.bfloat16
F32 = jnp.float32


def _params(*sem):
    return pltpu.CompilerParams(dimension_semantics=sem, vmem_limit_bytes=VMEM_LIMIT)


def _ln(v, eps):
    mu = jnp.mean(v, axis=-1, keepdims=True)
    d = v - mu
    var = jnp.mean(d * d, axis=-1, keepdims=True)
    return d * lax.rsqrt(var + eps)


def _rms(v, g):
    return v * lax.rsqrt(jnp.mean(v * v, axis=-1, keepdims=True) + RMS_EPS) * g


def _dot(a, b):
    return jnp.dot(a, b, preferred_element_type=F32)


def _store_row_tiles(ref, base, val):
    rows = val.shape[0]
    for ch in range(ROW_CHUNKS):
        ref[pl.ds(base + ch, rows, stride=ROW_CHUNKS), :] = val[:, ch * LANES:(ch + 1) * LANES]


def _load_row_tiles(ref, base, rows):
    return jnp.concatenate(
        [ref[pl.ds(base + ch, rows, stride=ROW_CHUNKS), :] for ch in range(ROW_CHUNKS)], axis=1)


def _mod_kernel(c_ref, w_ref, b_ref, o_ref):
    c = c_ref[...]
    cond = c * jax.nn.sigmoid(c)
    o_ref[...] = _dot(cond, w_ref[...]) + b_ref[...]


def _modulation(c, ada_w, ada_b):
    tn = 1536
    n = 6 * D_MODEL
    c8 = jnp.zeros((8, D_MODEL), F32).at[:BATCH].set(c)
    out = pl.pallas_call(
        _mod_kernel,
        grid=(DEPTH, n // tn),
        in_specs=[
            pl.BlockSpec((8, D_MODEL), lambda l, j: (0, 0)),
            pl.BlockSpec((None, D_MODEL, tn), lambda l, j: (l, 0, j)),
            pl.BlockSpec((None, 1, tn), lambda l, j: (l, 0, j)),
        ],
        out_specs=pl.BlockSpec((None, 8, tn), lambda l, j: (l, 0, j)),
        out_shape=jax.ShapeDtypeStruct((DEPTH, 8, n), F32),
        compiler_params=_params("arbitrary", "arbitrary"),
        name="adaln_mod",
    )(c8, ada_w, ada_b.reshape(DEPTH, 1, n))
    return out[:, :BATCH].reshape(DEPTH, BATCH, 6, 1, D_MODEL)


def _mod_spec(chunk, tile):
    return pl.BlockSpec((None, None, 1, D_MODEL),
                        lambda i: ((i * tile) // SEQ, chunk, 0, 0))


def _rope(t, tab):
    c, s1, s2 = tab[:, :LANES], tab[:, LANES:2 * LANES], tab[:, 2 * LANES:]
    outs = []
    for h in range(MLA_HEADS):
        th = t[:, h * HEAD_PAD:(h + 1) * HEAD_PAD]
        outs.append(th * c + pltpu.roll(th, HEAD_PAD - QK_ROPE // 2, 1) * s1
                    + pltpu.roll(th, QK_ROPE // 2, 1) * s2)
    return jnp.concatenate(outs, axis=1)


def _pre0_kernel(x_ref, sc_ref, sh_ref, rope_ref, win_ref, gq_ref, gkv_ref, wuq_ref, wk_ref,
                 wuv_ref, dft_ref, q_ref, k_ref, v_ref, a_ref, b_ref):
    h = _ln(x_ref[...], MOD_EPS) * (1.0 + sc_ref[...]) + sh_ref[...]
    p = _dot(h.astype(BF16), win_ref[...])
    c_q = p[:, :Q_RANK]
    c_kv = p[:, Q_RANK:Q_RANK + KV_RANK]
    u_f = p[:, Q_RANK + KV_RANK:Q_RANK + KV_RANK + FNET_WIDTH]
    k_r = p[:, Q_RANK + KV_RANK + FNET_WIDTH:]
    tab = rope_ref[...]
    q = _dot(_rms(c_q, gq_ref[...]).astype(BF16), wuq_ref[...]) * (QK_SCALE * LOG2E)
    q_ref[...] = _rope(q, tab).astype(BF16)
    ckv = _rms(c_kv, gkv_ref[...]).astype(BF16)
    kin = jnp.concatenate([ckv, k_r.astype(BF16)], axis=1)
    k_ref[...] = _rope(_dot(kin, wk_ref[...]), tab).astype(BF16)
    v_ref[...] = _dot(ckv, wuv_ref[...]).astype(BF16)
    ab = _dot(u_f.astype(BF16), dft_ref[...])
    a_ref[...] = ab[:, :FNET_WIDTH].astype(BF16)
    b_ref[...] = ab[:, FNET_WIDTH:].astype(BF16)


def _channel_dft():
    n = np.arange(FNET_GROUP_DIM)
    ang = 2.0 * np.pi * ((n[:, None] * n[None, :]) % FNET_GROUP_DIM) / FNET_GROUP_DIM
    norm = 1.0 / math.sqrt(SEQ * FNET_GROUP_DIM)
    eye = np.eye(FNET_GROUPS)
    cc = np.kron(eye, np.cos(ang) * norm)
    ss = np.kron(eye, np.sin(ang) * norm)
    return jnp.asarray(np.concatenate([cc, ss], axis=1), BF16)


def _pre0(x, modl, rope_tab, w_in, gq, gkv, w_uq, w_uk, w_uv):
    tm = ROW_TILE
    nq = Q_RANK + KV_RANK
    w_in_r = jnp.concatenate(
        [w_in[:, :nq], w_in[:, nq + QK_ROPE:], w_in[:, nq:nq + QK_ROPE],
         jnp.zeros((D_MODEL, LANES - QK_ROPE), F32)], axis=1).astype(BF16)
    wuq_p = jnp.pad(w_uq.reshape(Q_RANK, MLA_HEADS, QK_NOPE + QK_ROPE),
                    ((0, 0), (0, 0), (0, HEAD_PAD - QK_NOPE - QK_ROPE)))
    wuq_p = wuq_p.reshape(Q_RANK, MLA_HEADS * HEAD_PAD).astype(BF16)
    wuk_p = jnp.pad(w_uk.reshape(KV_RANK, MLA_HEADS, QK_NOPE),
                    ((0, 0), (0, 0), (0, HEAD_PAD - QK_NOPE))).reshape(KV_RANK, MLA_HEADS * HEAD_PAD)
    place = np.zeros((LANES, MLA_HEADS, HEAD_PAD), np.float32)
    for j in range(QK_ROPE):
        place[j, :, QK_NOPE + j] = 1.0
    wk_p = jnp.concatenate([wuk_p, jnp.asarray(place.reshape(LANES, -1))], axis=0).astype(BF16)
    full = lambda shape: pl.BlockSpec(shape, lambda i: (0,) * len(shape))
    nst = SEQ // tm
    wide = MLA_HEADS * HEAD_PAD
    return pl.pallas_call(
        _pre0_kernel,
        grid=(TOKENS // tm,),
        in_specs=[
            pl.BlockSpec((tm, D_MODEL), lambda i: (i, 0)),
            _mod_spec(1, tm), _mod_spec(0, tm),
            pl.BlockSpec((tm, 3 * LANES), lambda i: (i, 0)),
            full((D_MODEL, D_MODEL)), full((1, Q_RANK)), full((1, KV_RANK)),
            full((Q_RANK, wide)), full((2 * LANES, wide)), full((KV_RANK, MLA_OUT)),
            full((FNET_WIDTH, 2 * FNET_WIDTH)),
        ],
        out_specs=[
            pl.BlockSpec((tm, wide), lambda i: (i, 0)),
            pl.BlockSpec((tm, wide), lambda i: (i, 0)),
            pl.BlockSpec((tm, MLA_OUT), lambda i: (i, 0)),
            pl.BlockSpec((tm, FNET_WIDTH), lambda i: (i % nst, i // nst)),
            pl.BlockSpec((tm, FNET_WIDTH), lambda i: (i % nst, i // nst)),
        ],
        out_shape=[
            jax.ShapeDtypeStruct((TOKENS, wide), BF16),
            jax.ShapeDtypeStruct((TOKENS, wide), BF16),
            jax.ShapeDtypeStruct((TOKENS, MLA_OUT), BF16),
            jax.ShapeDtypeStruct((SEQ, BATCH * FNET_WIDTH), BF16),
            jax.ShapeDtypeStruct((SEQ, BATCH * FNET_WIDTH), BF16),
        ],
        compiler_params=_params("arbitrary"),
        name="mla_fnet_front",
    )(x, modl, modl, rope_tab, w_in_r, gq.reshape(1, -1), gkv.reshape(1, -1), wuq_p, wk_p,
      w_uv.astype(BF16), _channel_dft())


def _attn_kernel(q_ref, k_ref, v_ref, o_ref):
    outs = []
    for hh in range(2):
        q = q_ref[:, hh * HEAD_PAD:(hh + 1) * HEAD_PAD]
        k = k_ref[:, hh * HEAD_PAD:(hh + 1) * HEAD_PAD]
        s = lax.dot_general(q, k, (((1,), (1,)), ((), ())), preferred_element_type=F32)
        m = jnp.max(s, axis=-1, keepdims=True)
        p = jnp.exp2(s - m)
        l = jnp.sum(p, axis=-1, keepdims=True)
        outs.append(_dot(p.astype(BF16), v_ref[...]) / l)
    lane = lax.broadcasted_iota(jnp.int32, outs[0].shape, 1)
    o_ref[...] = jnp.where(lane < V_HEAD, outs[0], outs[1]).astype(BF16)


def _attention(q, k, v):
    tq = ATT_Q_TILE
    wide = MLA_HEADS * HEAD_PAD
    q = q.reshape(BATCH, SEQ, wide)
    k = k.reshape(BATCH, SEQ, wide)
    v = v.reshape(BATCH, SEQ, MLA_OUT)
    out = pl.pallas_call(
        _attn_kernel,
        grid=(BATCH, MLA_HEADS // 2, SEQ // tq),
        in_specs=[
            pl.BlockSpec((None, tq, 2 * HEAD_PAD), lambda b, h, i: (b, i, h)),
            pl.BlockSpec((None, SEQ, 2 * HEAD_PAD), lambda b, h, i: (b, 0, h)),
            pl.BlockSpec((None, SEQ, 2 * V_HEAD), lambda b, h, i: (b, 0, h)),
        ],
        out_specs=pl.BlockSpec((None, tq, 2 * V_HEAD), lambda b, h, i: (b, i, h)),
        out_shape=jax.ShapeDtypeStruct((BATCH, SEQ, MLA_OUT), BF16),
        compiler_params=_params("arbitrary", "arbitrary", "arbitrary"),
        name="mla_attention",
    )(q, k, v)
    return out.reshape(TOKENS, MLA_OUT)


def _fnet_kernel(tc_ref, ts_ref, c0_ref, s0_ref, a_ref, b_ref, o_ref, acc_ref):
    kk = pl.program_id(1)

    @pl.when(kk == 0)
    def _():
        acc_ref[...] = jnp.zeros_like(acc_ref)

    tc, ts, c0, s0 = tc_ref[...], ts_ref[...], c0_ref[...], s0_ref[...]
    cs = (c0 * tc - s0 * ts).astype(BF16)
    sn = (-s0 * tc - c0 * ts).astype(BF16)
    acc_ref[...] += _dot(cs, a_ref[...]) + _dot(sn, b_ref[...])

    @pl.when(kk == pl.num_programs(1) - 1)
    def _():
        o_ref[...] = acc_ref[...].astype(BF16)


def _dft_tables(tk):
    def cos_sin(k, m):
        ang = (2.0 * np.pi / SEQ) * ((k * m) % SEQ).astype(F32)
        return jnp.cos(ang), jnp.sin(ang)

    k = jnp.arange(SEQ, dtype=jnp.int32)
    tc, ts = cos_sin(k[:, None], jnp.arange(tk, dtype=jnp.int32)[None, :])
    c0, s0 = cos_sin(k[None, :], (jnp.arange(SEQ // tk, dtype=jnp.int32) * tk)[:, None])
    return tc, ts, c0[:, :, None], s0[:, :, None]


def _fnet(a_t, b_t):
    tm, tk = FNET_M_TILE, FNET_K_TILE
    n = BATCH * FNET_WIDTH
    tc, ts, c0, s0 = _dft_tables(tk)
    return pl.pallas_call(
        _fnet_kernel,
        grid=(SEQ // tm, SEQ // tk),
        in_specs=[
            pl.BlockSpec((tm, tk), lambda i, kk: (i, 0)),
            pl.BlockSpec((tm, tk), lambda i, kk: (i, 0)),
            pl.BlockSpec((None, tm, 1), lambda i, kk: (kk, i, 0)),
            pl.BlockSpec((None, tm, 1), lambda i, kk: (kk, i, 0)),
            pl.BlockSpec((tk, n), lambda i, kk: (kk, 0)),
            pl.BlockSpec((tk, n), lambda i, kk: (kk, 0)),
        ],
        out_specs=pl.BlockSpec((tm, n), lambda i, kk: (i, 0)),
        out_shape=jax.ShapeDtypeStruct((SEQ, n), BF16),
        scratch_shapes=[pltpu.VMEM((tm, n), F32)],
        compiler_params=_params("arbitrary", "arbitrary"),
        name="fnet_seq_dft",
    )(tc, ts, c0, s0, a_t, b_t)


def _top4(logits):
    rows = logits.shape[0]
    lane = lax.broadcasted_iota(jnp.int32, (rows, LANES), 1).astype(F32)
    work = logits
    vals, idxs = [], []
    for _ in range(TOP_K):
        m = jnp.max(work, axis=-1, keepdims=True)
        idx = jnp.min(jnp.where(work == m, lane, float(LANES)), axis=-1, keepdims=True)
        vals.append(m)
        idxs.append(idx.astype(jnp.int32))
        work = jnp.where(lane == idx, -jnp.inf, work)
    es = [jnp.exp(v - vals[0]) for v in vals]
    den = es[0] + es[1] + es[2] + es[3]
    lane8 = lax.broadcasted_iota(jnp.int32, (rows, 8), 1)
    top_e = jnp.zeros((rows, 8), jnp.int32)
    gates = jnp.zeros((rows, 8), F32)
    for kk in range(TOP_K):
        top_e = jnp.where(lane8 == kk, idxs[kk], top_e)
        gates = jnp.where(lane8 == kk, es[kk] / den, gates)
    return top_e, gates


def _mixer_tail(o, x_ref, gm_ref, scf_ref, shf_ref, lng_ref, lnb_ref, wr_ref, br_ref,
                xo_ref, h_ref, te_ref, gt_ref):
    xn = _ln(DEEPNORM_ALPHA * x_ref[...] + gm_ref[...] * o, LN_EPS) * lng_ref[...] + lnb_ref[...]
    xo_ref[...] = xn
    h = _ln(xn, MOD_EPS) * (1.0 + scf_ref[...]) + shf_ref[...]
    _store_row_tiles(h_ref, 0, h)
    logits = _dot(h, wr_ref[...]) + br_ref[...]
    top_e, gates = _top4(logits)
    te_ref[...] = top_e
    gt_ref[...] = gates


def _post0_kernel(oa_ref, ob_ref, wa_ref, wb_ref, *rest):
    o = _dot(oa_ref[...], wa_ref[...]) + _dot(ob_ref[...], wb_ref[...])
    _mixer_tail(o, *rest)


def _post1_kernel(u_ref, up_ref, un_ref, gb_ref, cw_ref, wo_ref, *rest):
    tm = u_ref.shape[0]
    i = pl.program_id(0)
    s0 = (i * tm) % SEQ
    u = u_ref[...].astype(F32)
    prev_row = jnp.where(s0 > 0, up_ref[...].astype(F32)[15:16, :], 0.0)
    next_row = jnp.where(s0 + tm < SEQ, un_ref[...].astype(F32)[0:1, :], 0.0)
    row = lax.broadcasted_iota(jnp.int32, u.shape, 0)
    u_m1 = jnp.where(row == 0, prev_row, pltpu.roll(u, 1, 0))
    u_p1 = jnp.where(row == tm - 1, next_row, pltpu.roll(u, tm - 1, 0))
    cw = cw_ref[...]
    y = u_m1 * cw[0:1, :] + u * cw[1:2, :] + u_p1 * cw[2:3, :]
    g = gb_ref[...].astype(F32) * y
    o = _dot(g.astype(BF16), wo_ref[...])
    _mixer_tail(o, *rest)


def _tail_specs(tm):
    row = lambda w: pl.BlockSpec((tm, w), lambda i: (i, 0))
    full = lambda shape: pl.BlockSpec(shape, lambda i: (0,) * len(shape))
    in_specs = [row(D_MODEL), _mod_spec(2, tm), _mod_spec(4, tm), _mod_spec(3, tm),
                full((1, D_MODEL)), full((1, D_MODEL)), full((D_MODEL, LANES)), full((1, LANES))]
    out_specs = [row(D_MODEL), pl.BlockSpec((tm * ROW_CHUNKS, LANES), lambda i: (i, 0)),
                 row(8), row(8)]
    out_shape = [jax.ShapeDtypeStruct((TOKENS, D_MODEL), F32),
                 jax.ShapeDtypeStruct((TOKENS * ROW_CHUNKS, LANES), F32),
                 jax.ShapeDtypeStruct((TOKENS, 8), jnp.int32),
                 jax.ShapeDtypeStruct((TOKENS, 8), F32)]
    return in_specs, out_specs, out_shape


def _tail_args(x, modl, ln_g, ln_b, w_router, b_router):
    wr = jnp.pad(w_router, ((0, 0), (0, LANES - N_EXPERTS)))
    br = jnp.concatenate([b_router, jnp.full((LANES - N_EXPERTS,), NEG_BIG, F32)]).reshape(1, LANES)
    return (x, modl, modl, modl, ln_g.reshape(1, -1), ln_b.reshape(1, -1), wr, br)


def _post0(o_a, o_b, w_out, x, modl, ln_g, ln_b, w_router, b_router):
    tm = ROW_TILE
    nst = SEQ // tm
    tin, tout, tshape = _tail_specs(tm)
    w = w_out.astype(BF16)
    full = lambda shape: pl.BlockSpec(shape, lambda i: (0,) * len(shape))
    return pl.pallas_call(
        _post0_kernel,
        grid=(TOKENS // tm,),
        in_specs=[pl.BlockSpec((tm, MLA_OUT), lambda i: (i, 0)),
                  pl.BlockSpec((tm, FNET_WIDTH), lambda i: (i % nst, i // nst)),
                  full((MLA_OUT, D_MODEL)), full((FNET_WIDTH, D_MODEL))] + tin,
        out_specs=tout,
        out_shape=tshape,
        compiler_params=_params("arbitrary"),
        name="mix_a_tail",
    )(o_a, o_b, w[:MLA_OUT], w[MLA_OUT:], *_tail_args(x, modl, ln_g, ln_b, w_router, b_router))


def _convin_kernel(x_ref, sc_ref, sh_ref, w_ref, gb_ref, u_ref):
    h = _ln(x_ref[...], MOD_EPS) * (1.0 + sc_ref[...]) + sh_ref[...]
    p = _dot(h.astype(BF16), w_ref[...])
    gb_ref[...] = p[:, :D_MODEL].astype(BF16)
    u_ref[...] = (p[:, D_MODEL:2 * D_MODEL] * p[:, 2 * D_MODEL:]).astype(BF16)


def _convin(x, modl, w_in):
    tm = ROW_TILE
    row = pl.BlockSpec((tm, D_MODEL), lambda i: (i, 0))
    return pl.pallas_call(
        _convin_kernel,
        grid=(TOKENS // tm,),
        in_specs=[row, _mod_spec(1, tm), _mod_spec(0, tm),
                  pl.BlockSpec((D_MODEL, 3 * D_MODEL), lambda i: (0, 0))],
        out_specs=[row, row],
        out_shape=[jax.ShapeDtypeStruct((TOKENS, D_MODEL), BF16)] * 2,
        compiler_params=_params("arbitrary"),
        name="conv_front",
    )(x, modl, modl, w_in.astype(BF16))


def _post1(u, gate_b, conv_w, w_out, x, modl, ln_g, ln_b, w_router, b_router):
    tm = ROW_TILE
    halo = 16
    tin, tout, tshape = _tail_specs(tm)
    nh = TOKENS // halo
    cw = jnp.zeros((8, D_MODEL), F32).at[:CONV_WIDTH].set(conv_w)
    row = pl.BlockSpec((tm, D_MODEL), lambda i: (i, 0))
    return pl.pallas_call(
        _post1_kernel,
        grid=(TOKENS // tm,),
        in_specs=[row,
                  pl.BlockSpec((halo, D_MODEL), lambda i: (jnp.maximum(i * (tm // halo) - 1, 0), 0)),
                  pl.BlockSpec((halo, D_MODEL),
                               lambda i: (jnp.minimum((i + 1) * (tm // halo), nh - 1), 0)),
                  row,
                  pl.BlockSpec((8, D_MODEL), lambda i: (0, 0)),
                  pl.BlockSpec((D_MODEL, D_MODEL), lambda i: (0, 0))] + tin,
        out_specs=tout,
        out_shape=tshape,
        compiler_params=_params("arbitrary"),
        name="conv_tail",
    )(u, u, u, gate_b, cw, w_out.astype(BF16),
      *_tail_args(x, modl, ln_g, ln_b, w_router, b_router))


def _lane_cumsum(v):
    lane = lax.broadcasted_iota(jnp.int32, v.shape, 1)
    s = 1
    while s < N_EXPERTS:
        v = v + jnp.where(lane >= s, pltpu.roll(v, s, 1), 0.0)
        s *= 2
    return v


def _rank_kernel(te_ref, dest_ref, meta_ref, cnt_ref, carry_ref, pstart_ref):
    ph = pl.program_id(0)
    i = pl.program_id(1)
    tb = te_ref.shape[0]
    lane = lax.broadcasted_iota(jnp.int32, (tb, LANES), 1)
    te = te_ref[...]
    onehot = [lane == te[:, kk:kk + 1] for kk in range(TOP_K)]
    msum = sum(oh.astype(F32) for oh in onehot)
    colsum = jnp.sum(msum, axis=0, keepdims=True)

    @pl.when((ph == 0) & (i == 0))
    def _():
        cnt_ref[...] = jnp.zeros_like(cnt_ref)

    @pl.when(ph == 0)
    def _():
        cnt_ref[...] += jnp.broadcast_to(colsum, cnt_ref.shape)

    @pl.when((ph == 1) & (i == 0))
    def _():
        cnt = cnt_ref[...]
        padded = jnp.floor((cnt + (MOE_BLOCK - 1)) * (1.0 / MOE_BLOCK)) * MOE_BLOCK
        pend = _lane_cumsum(padded)
        pstart_ref[...] = pend - padded
        carry_ref[...] = jnp.zeros_like(carry_ref)
        lane8 = lax.broadcasted_iota(jnp.int32, (8, LANES), 1)
        row8 = lax.broadcasted_iota(jnp.int32, (8, LANES), 0)
        thr = ((row8 * LANES + lane8) * MOE_BLOCK).astype(F32)
        blk = jnp.zeros((8, LANES), F32)
        for e in range(N_EXPERTS):
            pe = jnp.sum(jnp.where(lane8 == e, pend, 0.0), axis=1, keepdims=True)
            blk = blk + (pe <= thr).astype(F32)
        blk = jnp.minimum(blk, N_EXPERTS - 1.0)
        total = jnp.sum(jnp.where(lane8 == N_EXPERTS - 1, pend, 0.0), axis=1, keepdims=True)
        nact = jnp.broadcast_to(total * (1.0 / MOE_BLOCK), (8, LANES))
        info = jnp.where(row8 == 0, cnt, jnp.where(row8 == 1, pend, nact))
        meta_ref[0:8, :] = blk.astype(jnp.int32)
        meta_ref[8:16, :] = info.astype(jnp.int32)

    @pl.when(ph == 1)
    def _():
        r = lax.broadcasted_iota(jnp.int32, (tb, tb), 0)
        c = lax.broadcasted_iota(jnp.int32, (tb, tb), 1)
        lower = (c < r).astype(BF16)
        prefix = _dot(lower, msum.astype(BF16))
        base = prefix + carry_ref[0:1, :] + pstart_ref[0:1, :]
        lane8 = lax.broadcasted_iota(jnp.int32, (tb, 8), 1)
        dest = jnp.zeros((tb, 8), jnp.int32)
        for kk in range(TOP_K):
            dk = jnp.sum(jnp.where(onehot[kk], base, 0.0), axis=1, keepdims=True)
            dest = jnp.where(lane8 == kk, dk.astype(jnp.int32), dest)
        dest_ref[...] = dest
        carry_ref[...] += jnp.broadcast_to(colsum, carry_ref.shape)


def _rank(top_e):
    tb = RANK_TILE
    dest, meta = pl.pallas_call(
        _rank_kernel,
        grid=(2, TOKENS // tb),
        in_specs=[pl.BlockSpec((tb, 8), lambda ph, i: (i, 0))],
        out_specs=[pl.BlockSpec((tb, 8), lambda ph, i: (i * ph, 0)),
                   pl.BlockSpec((16, LANES), lambda ph, i: (0, 0))],
        out_shape=[jax.ShapeDtypeStruct((TOKENS, 8), jnp.int32),
                   jax.ShapeDtypeStruct((16, LANES), jnp.int32)],
        scratch_shapes=[pltpu.VMEM((8, LANES), F32)] * 3,
        compiler_params=_params("arbitrary", "arbitrary"),
        name="moe_rank",
    )(top_e)
    dest_flat = dest[:, :TOP_K].reshape(-1)
    block_e = meta[0:8].reshape(-1)[:MOE_NBLOCKS]
    counts = meta[8, :N_EXPERTS]
    pend = meta[9, :N_EXPERTS]
    nact = meta[10, 0:1]
    return dest_flat, block_e, counts, pend, nact


def _slot_kernel(pend_ref, cnt_ref, nact_ref, dest_ref, slot_ref):
    i = pl.program_id(0)
    tc = dest_ref.shape[0] // TOP_K

    def fill_pad(s, carry):
        slot_ref[s + MOE_BLOCK] = TOP_K * TOKENS + (s & (2 * MOE_BLOCK - 1))
        return carry

    @pl.when(i == 0)
    def _():
        lax.fori_loop(-MOE_BLOCK, 0, fill_pad, 0)
        for e in range(N_EXPERTS):
            lax.fori_loop((pend_ref[e - 1] if e else 0) + cnt_ref[e], pend_ref[e], fill_pad, 0)
        lax.fori_loop(nact_ref[0] * MOE_BLOCK, MOE_ROWS, fill_pad, 0)

    t0 = i * tc

    def scatter(r, carry):
        for kk in range(TOP_K):
            slot_ref[dest_ref[r * TOP_K + kk] + MOE_BLOCK] = kk * TOKENS + t0 + r
        return carry

    lax.fori_loop(0, tc, scatter, 0, unroll=8)


def _slot_table(dest_flat, counts, pend, nact):
    tc = 4096
    return pl.pallas_call(
        _slot_kernel,
        grid_spec=pltpu.PrefetchScalarGridSpec(
            num_scalar_prefetch=3,
            grid=(TOKENS // tc,),
            in_specs=[pl.BlockSpec((tc * TOP_K,), lambda i, *_: (i,), memory_space=pltpu.SMEM)],
            out_specs=pl.BlockSpec(memory_space=pltpu.SMEM),
        ),
        out_shape=jax.ShapeDtypeStruct((MOE_ROWS + MOE_BLOCK,), jnp.int32),
        compiler_params=_params("arbitrary"),
        name="moe_slots",
    )(pend, counts, nact, dest_flat)


def _moe_kernel(layer, be_ref, nact_ref, pend_ref, slot_ref,
                h_ref, wg_hbm, wl_hbm, wd_hbm, bg0, bl0, bd0, bg1, bl1, bd1, o4_ref,
                rbuf, wstage, wbf, gsem, ssem, wsem):
    step = pl.program_id(0)
    nact = nact_ref[0]
    rows = MOE_BLOCK * ROW_CHUNKS
    weights = (wg_hbm, wl_hbm, wd_hbm)
    zero = be_ref[0] >> 16
    xoff = (0, rows)
    yoff = (2 * rows, 3 * rows)
    aoff = 4 * rows + zero

    def region(off):
        return rbuf.at[pl.ds(pl.multiple_of(off, ROW_CHUNKS), rows)]

    def tile_at(buf, first):
        return buf.at[pl.ds(pl.multiple_of(first, ROW_CHUNKS), ROW_CHUNKS)]

    def gather_row(blk, r, q):
        tok = slot_ref[(blk + 1) * MOE_BLOCK + r] & (TOKENS - 1)
        return pltpu.make_async_copy(tile_at(h_ref, tok * ROW_CHUNKS),
                                     tile_at(rbuf, xoff[q] + r * ROW_CHUNKS), gsem.at[q])

    def scatter_row(blk, r, q):
        row = slot_ref[(blk + 1) * MOE_BLOCK + r]
        return pltpu.make_async_copy(tile_at(rbuf, yoff[q] + r * ROW_CHUNKS),
                                     tile_at(o4_ref, row * ROW_CHUNKS), ssem.at[q])

    def gather_wait(q):
        pltpu.make_async_copy(h_ref.at[pl.ds(0, rows)], region(xoff[q]), gsem.at[q]).wait()

    def scatter_wait(q):
        pltpu.make_async_copy(region(yoff[q]), o4_ref.at[pl.ds(0, rows)], ssem.at[q]).wait()

    def weights_start(e):
        for i, w in enumerate(weights):
            pltpu.make_async_copy(w.at[layer, e], wstage.at[i], wsem.at[i]).start(priority=1)

    def weights_wait():
        for i, w in enumerate(weights):
            pltpu.make_async_copy(w.at[layer, 0], wstage.at[i], wsem.at[i]).wait()

    @pl.when(step == 0)
    def _():
        for q in range(2):
            rbuf[pl.ds((2 + q) * rows, rows), :] = jnp.zeros((rows, LANES), F32)
            dump = o4_ref.at[pl.ds(TOP_K * TOKENS * ROW_CHUNKS + q * rows, rows)]
            cp = pltpu.make_async_copy(region(yoff[q]), dump, ssem.at[q])
            cp.start()
            cp.wait()

        def first_rows(r, carry):
            gather_row(0, r, 0).start()
            return carry

        lax.fori_loop(0, MOE_BLOCK, first_rows, 0, unroll=8)
        weights_start(be_ref[0])

    def run_block(b, q, bg_ref, bl_ref, bd_ref):
        o = 1 - q

        @pl.when(b < nact)
        def _():
            e = be_ref[b]

            @pl.when((b == 0) | (e != be_ref[jnp.maximum(b - 1, 0)]))
            def _():
                weights_wait()
                for i in range(3):
                    wbf[i] = wstage[i].astype(BF16)
                nxt = pend_ref[e] // MOE_BLOCK

                @pl.when(nxt < nact)
                def _():
                    weights_start(be_ref[jnp.minimum(nxt, MOE_NBLOCKS - 1)])

            gather_wait(q)

            @pl.when(b >= 1)
            def _():
                scatter_wait(q)

            x = _load_row_tiles(rbuf, xoff[q] + zero, MOE_BLOCK).astype(BF16)
            nb = jnp.minimum(b + 1, nact - 1)
            for r in range(MOE_BLOCK):
                gather_row(nb, r, o).start()
            for r in range(MOE_BLOCK):
                scatter_row(b - 1, r, o).start(priority=r % 2)

            g = jnp.minimum(_dot(x, wbf[0]) + bg_ref[...], SWIGLU_LIMIT)
            lin = jnp.clip(_dot(x, wbf[1]) + bl_ref[...], -SWIGLU_LIMIT, SWIGLU_LIMIT)
            a = g * jax.nn.sigmoid(SWIGLU_ALPHA * g) * (lin + 1.0)
            for ch in range(ROW_CHUNKS):
                rbuf[pl.ds(aoff + ch * MOE_BLOCK, MOE_BLOCK), :] = a[:, ch * LANES:(ch + 1) * LANES]
            a = jnp.concatenate([rbuf[pl.ds(aoff + ch * MOE_BLOCK, MOE_BLOCK), :]
                                 for ch in range(ROW_CHUNKS)], axis=1)
            _store_row_tiles(rbuf, yoff[q] + zero, _dot(a.astype(BF16), wbf[2]) + bd_ref[...])

            @pl.when(b == nact - 1)
            def _():
                def last_rows(r, carry):
                    scatter_row(b, r, q).start()
                    return carry

                lax.fori_loop(0, MOE_BLOCK, last_rows, 0, unroll=8)
                gather_wait(o)
                scatter_wait(o)
                scatter_wait(q)

    run_block(2 * step, 0, bg0, bl0, bd0)
    run_block(2 * step + 1, 1, bg1, bl1, bd1)


def _moe_experts(l, h_tiles, block_e, nact, pend, slot_row, w_glu, b_glu, w_lin, b_lin, w_down, b_down):
    rows = MOE_BLOCK * ROW_CHUNKS

    def bspec(q):
        return pl.BlockSpec((None, None, 1, D_MODEL),
                            lambda s, be, na, pe, sl: (l, be[jnp.minimum(2 * s + q, na[0] - 1)], 0, 0))

    b3 = lambda b: b.reshape(DEPTH, N_EXPERTS, 1, D_MODEL)
    hbm = pl.BlockSpec(memory_space=pl.ANY)
    biases = (b3(b_glu), b3(b_lin), b3(b_down))
    return pl.pallas_call(
        functools.partial(_moe_kernel, l),
        grid_spec=pltpu.PrefetchScalarGridSpec(
            num_scalar_prefetch=4,
            grid=(MOE_NBLOCKS // 2,),
            in_specs=[hbm, hbm, hbm, hbm] + [bspec(0)] * 3 + [bspec(1)] * 3,
            out_specs=hbm,
            scratch_shapes=[
                pltpu.VMEM((5 * rows, LANES), F32),
                pltpu.VMEM((3, D_MODEL, D_MODEL), F32),
                pltpu.VMEM((3, D_MODEL, D_MODEL), BF16),
                pltpu.SemaphoreType.DMA((2,)), pltpu.SemaphoreType.DMA((2,)),
                pltpu.SemaphoreType.DMA((3,))],
        ),
        out_shape=jax.ShapeDtypeStruct(((TOP_K * TOKENS + 2 * MOE_BLOCK) * ROW_CHUNKS, LANES), F32),
        compiler_params=_params("arbitrary"),
        name="moe_experts",
    )(block_e, nact, pend, slot_row, h_tiles, w_glu, w_lin, w_down, *biases, *biases)


def _combine_kernel(y0_ref, y1_ref, y2_ref, y3_ref, gt_ref, x_ref, gf_ref, lng_ref, lnb_ref, xo_ref):
    tm = x_ref.shape[0]
    gt = gt_ref[...]
    o = gt[:, 0:1] * _load_row_tiles(y0_ref, 0, tm)
    for kk, y_ref in enumerate((y1_ref, y2_ref, y3_ref), start=1):
        o = o + gt[:, kk:kk + 1] * _load_row_tiles(y_ref, 0, tm)
    xo_ref[...] = (_ln(DEEPNORM_ALPHA * x_ref[...] + gf_ref[...] * o, LN_EPS) * lng_ref[...]
                   + lnb_ref[...])


def _combine(o4, gates, x, modl, ln_g, ln_b):
    tm = ROW_TILE
    row = pl.BlockSpec((tm, D_MODEL), lambda i: (i, 0))
    vec = pl.BlockSpec((1, D_MODEL), lambda i: (0, 0))
    nt = TOKENS // tm
    yspec = lambda kk: pl.BlockSpec((tm * ROW_CHUNKS, LANES), lambda i: (kk * nt + i, 0))
    return pl.pallas_call(
        _combine_kernel,
        grid=(nt,),
        in_specs=[yspec(0), yspec(1), yspec(2), yspec(3),
                  pl.BlockSpec((tm, 8), lambda i: (i, 0)),
                  row, _mod_spec(5, tm), vec, vec],
        out_specs=row,
        out_shape=jax.ShapeDtypeStruct((TOKENS, D_MODEL), F32),
        compiler_params=_params("arbitrary"),
        name="moe_combine",
    )(o4, o4, o4, o4, gates, x, modl, ln_g.reshape(1, -1), ln_b.reshape(1, -1))


def _rope_table(positions):
    half = QK_ROPE // 2
    inv_freq = ROPE_THETA ** (-jnp.arange(0, QK_ROPE, 2, dtype=F32) / QK_ROPE)
    ang = positions.astype(F32)[..., None] * inv_freq
    cos, sin = jnp.cos(ang), jnp.sin(ang)
    z = lambda n: jnp.zeros(ang.shape[:-1] + (n,), F32)
    tail = HEAD_PAD - QK_NOPE - QK_ROPE
    c = jnp.concatenate([jnp.ones(ang.shape[:-1] + (QK_NOPE,), F32), cos, cos, z(tail)], -1)
    s1 = jnp.concatenate([z(QK_NOPE), -sin, z(half), z(tail)], -1)
    s2 = jnp.concatenate([z(QK_NOPE), z(half), sin, z(tail)], -1)
    return jnp.concatenate([c, s1, s2], -1).reshape(TOKENS, 3 * LANES)


def _moe_layer(l, x, h, top_e, gates, modl, ln_g, ln_b, w_glu, b_glu, w_lin, b_lin, w_down, b_down):
    dest_flat, block_e, counts, pend, nact = _rank(top_e)
    slot_row = _slot_table(dest_flat, counts, pend, nact)
    o4 = _moe_experts(l, h, block_e, nact, pend, slot_row, w_glu, b_glu, w_lin, b_lin, w_down, b_down)
    return _combine(o4, gates, x, modl, ln_g[l], ln_b[l])


def kernel(x, c, positions, ada_w, ada_b, ln_mix_g, ln_mix_b, ln_ffn_g, ln_ffn_b, mla_w_in, mla_q_norm_g, mla_kv_norm_g, mla_w_uq, mla_w_uk, mla_w_uv, mix_a_w_out, conv_w_in, conv_w, conv_w_out, moe_w_router, moe_b_router, moe_w_glu, moe_b_glu, moe_w_lin, moe_b_lin, moe_w_down, moe_b_down):
    mod = _modulation(c, ada_w, ada_b)
    rope_tab = _rope_table(positions)
    x = x.reshape(TOKENS, D_MODEL)
    for l in range(DEPTH):
        modl = mod[l]
        i = l // 2
        if l % 2 == 0:
            q, k, v, a_t, b_t = _pre0(x, modl, rope_tab, mla_w_in[i], mla_q_norm_g[i],
                                      mla_kv_norm_g[i], mla_w_uq[i], mla_w_uk[i], mla_w_uv[i])
            o_a = _attention(q, k, v)
            o_b = _fnet(a_t, b_t)
            x, h, top_e, gates = _post0(o_a, o_b, mix_a_w_out[i], x, modl, ln_mix_g[l], ln_mix_b[l],
                                        moe_w_router[l], moe_b_router[l])
        else:
            gate_b, u = _convin(x, modl, conv_w_in[i])
            x, h, top_e, gates = _post1(u, gate_b, conv_w[i], conv_w_out[i], x, modl, ln_mix_g[l],
                                        ln_mix_b[l], moe_w_router[l], moe_b_router[l])
        x = _moe_layer(l, x, h, top_e, gates, modl, ln_ffn_g, ln_ffn_b, moe_w_glu, moe_b_glu,
                       moe_w_lin, moe_b_lin, moe_w_down, moe_b_down)
    return x.reshape(BATCH, SEQ, D_MODEL)
```

```python
import functools
import math

import numpy as np
import jax
import jax.numpy as jnp
from jax import lax
from jax.experimental import pallas as pl
from jax.experimental.pallas import tpu as pltpu

D_MODEL = 1024
BATCH = 4
SEQ = 4096
DEPTH = 2
TOKENS = BATCH * SEQ

MLA_HEADS = 8
QK_NOPE = 64
QK_ROPE = 32
V_HEAD = 64
Q_RANK = 256
KV_RANK = 128
ROPE_THETA = 10000.0
MLA_OUT = MLA_HEADS * V_HEAD
FNET_GROUPS = 8
FNET_GROUP_DIM = 64
FNET_WIDTH = FNET_GROUPS * FNET_GROUP_DIM
CONV_WIDTH = 3
N_EXPERTS = 32
TOP_K = 4
SWIGLU_LIMIT = 7.0
SWIGLU_ALPHA = 1.702
DEEPNORM_ALPHA = (2 * DEPTH) ** 0.25
LN_EPS = 1e-5
MOD_EPS = 1e-6
RMS_EPS = 1e-6
QK_SCALE = (QK_NOPE + QK_ROPE) ** -0.5
LOG2E = math.log2(math.e)

LANES = 128
HEAD_PAD = 128
ROW_CHUNKS = D_MODEL // LANES
VMEM_LIMIT = 56 * 1024 * 1024

ROW_TILE = 256
ATT_Q_TILE = 256
ATT_HEADS = 8
FNET_M_TILE = 1024
FNET_K_TILE = 512
RANK_TILE = 1024
MOE_BLOCK = 256
MOE_ROWS = TOKENS * TOP_K + N_EXPERTS * MOE_BLOCK
MOE_NBLOCKS = MOE_ROWS // MOE_BLOCK
NEG_BIG = -1e30

BF16 = jnp.bfloat16
F32 = jnp.float32


def _params(*sem):
    return pltpu.CompilerParams(dimension_semantics=sem, vmem_limit_bytes=VMEM_LIMIT)


def _ln(v, eps):
    mu = jnp.mean(v, axis=-1, keepdims=True)
    d = v - mu
    var = jnp.mean(d * d, axis=-1, keepdims=True)
    return d * lax.rsqrt(var + eps)


def _rms(v, g):
    return v * lax.rsqrt(jnp.mean(v * v, axis=-1, keepdims=True) + RMS_EPS) * g


def _dot(a, b):
    return jnp.dot(a, b, preferred_element_type=F32)


def _store_row_tiles(ref, base, val):
    rows = val.shape[0]
    for ch in range(ROW_CHUNKS):
        ref[pl.ds(base + ch, rows, stride=ROW_CHUNKS), :] = val[:, ch * LANES:(ch + 1) * LANES]


def _load_row_tiles(ref, base, rows):
    return jnp.concatenate(
        [ref[pl.ds(base + ch, rows, stride=ROW_CHUNKS), :] for ch in range(ROW_CHUNKS)], axis=1)


def _mod_kernel(c_ref, w_ref, b_ref, o_ref):
    c = c_ref[...]
    cond = c * jax.nn.sigmoid(c)
    o_ref[...] = _dot(cond, w_ref[...]) + b_ref[...]


def _modulation(c, ada_w, ada_b):
    tn = 1536
    n = 6 * D_MODEL
    c8 = jnp.zeros((8, D_MODEL), F32).at[:BATCH].set(c)
    out = pl.pallas_call(
        _mod_kernel,
        grid=(DEPTH, n // tn),
        in_specs=[
            pl.BlockSpec((8, D_MODEL), lambda l, j: (0, 0)),
            pl.BlockSpec((None, D_MODEL, tn), lambda l, j: (l, 0, j)),
            pl.BlockSpec((None, 1, tn), lambda l, j: (l, 0, j)),
        ],
        out_specs=pl.BlockSpec((None, 8, tn), lambda l, j: (l, 0, j)),
        out_shape=jax.ShapeDtypeStruct((DEPTH, 8, n), F32),
        compiler_params=_params("arbitrary", "arbitrary"),
        name="adaln_mod",
    )(c8, ada_w, ada_b.reshape(DEPTH, 1, n))
    return out[:, :BATCH].reshape(DEPTH, BATCH, 6, 1, D_MODEL)


def _mod_spec(chunk, tile):
    return pl.BlockSpec((None, None, 1, D_MODEL),
                        lambda i: ((i * tile) // SEQ, chunk, 0, 0))


def _rope(t, tab):
    c, s1, s2 = tab[:, :LANES], tab[:, LANES:2 * LANES], tab[:, 2 * LANES:]
    outs = []
    for h in range(MLA_HEADS):
        th = t[:, h * HEAD_PAD:(h + 1) * HEAD_PAD]
        outs.append(th * c + pltpu.roll(th, HEAD_PAD - QK_ROPE // 2, 1) * s1
                    + pltpu.roll(th, QK_ROPE // 2, 1) * s2)
    return jnp.concatenate(outs, axis=1)


def _pre0_kernel(x_ref, sc_ref, sh_ref, cs_ref, place_ref, bias_ref, win_ref, gq_ref, gkv_ref,
                 wuq_ref, wk_ref, wuv_ref, dft_ref, q_ref, k_ref, v_ref, a_ref, b_ref):
    h = _ln(x_ref[...], MOD_EPS) * (1.0 + sc_ref[...]) + sh_ref[...]
    p = _dot(h.astype(BF16), win_ref[...])
    c_q = p[:, :Q_RANK]
    c_kv = p[:, Q_RANK:Q_RANK + KV_RANK]
    u_f = p[:, Q_RANK + KV_RANK:Q_RANK + KV_RANK + FNET_WIDTH]
    k_r = p[:, Q_RANK + KV_RANK + FNET_WIDTH:]
    cs = cs_ref[...]
    cs_hi = cs.astype(BF16)
    cs_lo = (cs - cs_hi.astype(F32)).astype(BF16)
    tab = _dot(cs_hi, place_ref[...]) + _dot(cs_lo, place_ref[...]) + bias_ref[...]
    q = _dot(_rms(c_q, gq_ref[...]).astype(BF16), wuq_ref[...]) * (QK_SCALE * LOG2E)
    q_ref[...] = _rope(q, tab).astype(BF16)
    ckv = _rms(c_kv, gkv_ref[...]).astype(BF16)
    kin = jnp.concatenate([ckv, k_r.astype(BF16)], axis=1)
    k_ref[...] = _rope(_dot(kin, wk_ref[...]), tab).astype(BF16)
    v_ref[...] = _dot(ckv, wuv_ref[...]).astype(BF16)
    ab = _dot(u_f.astype(BF16), dft_ref[...])
    a_ref[...] = ab[:, :FNET_WIDTH].astype(BF16)
    b_ref[...] = ab[:, FNET_WIDTH:].astype(BF16)


def _channel_dft():
    n = np.arange(FNET_GROUP_DIM)
    ang = 2.0 * np.pi * ((n[:, None] * n[None, :]) % FNET_GROUP_DIM) / FNET_GROUP_DIM
    norm = 1.0 / math.sqrt(SEQ * FNET_GROUP_DIM)
    eye = np.eye(FNET_GROUPS)
    cc = np.kron(eye, np.cos(ang) * norm)
    ss = np.kron(eye, np.sin(ang) * norm)
    return jnp.asarray(np.concatenate([cc, ss], axis=1), BF16)


def _rope_placement():
    half = QK_ROPE // 2
    place = np.zeros((QK_ROPE, 3 * LANES), np.float32)
    bias = np.zeros((1, 3 * LANES), np.float32)
    bias[0, :QK_NOPE] = 1.0
    for j in range(half):
        place[j, QK_NOPE + j] = 1.0
        place[j, QK_NOPE + half + j] = 1.0
        place[half + j, LANES + QK_NOPE + j] = -1.0
        place[half + j, 2 * LANES + QK_NOPE + half + j] = 1.0
    return jnp.asarray(place, BF16), jnp.asarray(bias)


def _pre0(x, modl, rope_cs, w_in, gq, gkv, w_uq, w_uk, w_uv):
    tm = ROW_TILE
    nq = Q_RANK + KV_RANK
    w_in_r = jnp.concatenate(
        [w_in[:, :nq], w_in[:, nq + QK_ROPE:], w_in[:, nq:nq + QK_ROPE],
         jnp.zeros((D_MODEL, LANES - QK_ROPE), F32)], axis=1).astype(BF16)
    wuq_p = jnp.pad(w_uq.reshape(Q_RANK, MLA_HEADS, QK_NOPE + QK_ROPE),
                    ((0, 0), (0, 0), (0, HEAD_PAD - QK_NOPE - QK_ROPE)))
    wuq_p = wuq_p.reshape(Q_RANK, MLA_HEADS * HEAD_PAD).astype(BF16)
    wuk_p = jnp.pad(w_uk.reshape(KV_RANK, MLA_HEADS, QK_NOPE),
                    ((0, 0), (0, 0), (0, HEAD_PAD - QK_NOPE))).reshape(KV_RANK, MLA_HEADS * HEAD_PAD)
    place = np.zeros((LANES, MLA_HEADS, HEAD_PAD), np.float32)
    for j in range(QK_ROPE):
        place[j, :, QK_NOPE + j] = 1.0
    wk_p = jnp.concatenate([wuk_p, jnp.asarray(place.reshape(LANES, -1))], axis=0).astype(BF16)
    full = lambda shape: pl.BlockSpec(shape, lambda i: (0,) * len(shape))
    nst = SEQ // tm
    wide = MLA_HEADS * HEAD_PAD
    return pl.pallas_call(
        _pre0_kernel,
        grid=(TOKENS // tm,),
        in_specs=[
            pl.BlockSpec((tm, D_MODEL), lambda i: (i, 0)),
            _mod_spec(1, tm), _mod_spec(0, tm),
            pl.BlockSpec((tm, QK_ROPE), lambda i: (i, 0)),
            full((QK_ROPE, 3 * LANES)), full((1, 3 * LANES)),
            full((D_MODEL, D_MODEL)), full((1, Q_RANK)), full((1, KV_RANK)),
            full((Q_RANK, wide)), full((2 * LANES, wide)), full((KV_RANK, MLA_OUT)),
            full((FNET_WIDTH, 2 * FNET_WIDTH)),
        ],
        out_specs=[
            pl.BlockSpec((tm, wide), lambda i: (i, 0)),
            pl.BlockSpec((tm, wide), lambda i: (i, 0)),
            pl.BlockSpec((tm, MLA_OUT), lambda i: (i, 0)),
            pl.BlockSpec((tm, FNET_WIDTH), lambda i: (i % nst, i // nst)),
            pl.BlockSpec((tm, FNET_WIDTH), lambda i: (i % nst, i // nst)),
        ],
        out_shape=[
            jax.ShapeDtypeStruct((TOKENS, wide), BF16),
            jax.ShapeDtypeStruct((TOKENS, wide), BF16),
            jax.ShapeDtypeStruct((TOKENS, MLA_OUT), BF16),
            jax.ShapeDtypeStruct((SEQ, BATCH * FNET_WIDTH), BF16),
            jax.ShapeDtypeStruct((SEQ, BATCH * FNET_WIDTH), BF16),
        ],
        compiler_params=_params("arbitrary"),
        name="mla_fnet_front",
    )(x, modl, modl, rope_cs, *_rope_placement(), w_in_r, gq.reshape(1, -1), gkv.reshape(1, -1),
      wuq_p, wk_p, w_uv.astype(BF16), _channel_dft())


def _attn_kernel(q_ref, k_ref, v_ref, o_ref):
    outs = []
    for hh in range(ATT_HEADS):
        q = q_ref[:, hh * HEAD_PAD:(hh + 1) * HEAD_PAD]
        k = k_ref[:, hh * HEAD_PAD:(hh + 1) * HEAD_PAD]
        s = lax.dot_general(q, k, (((1,), (1,)), ((), ())), preferred_element_type=F32)
        m = jnp.max(s, axis=-1, keepdims=True)
        p = jnp.exp2(s - m)
        l = jnp.sum(p, axis=-1, keepdims=True)
        v = v_ref[:, (hh // 2) * LANES:(hh // 2 + 1) * LANES]
        outs.append(_dot(p.astype(BF16), v) / l)
    lane = lax.broadcasted_iota(jnp.int32, outs[0].shape, 1)
    pairs = [jnp.where(lane < V_HEAD, outs[2 * j], outs[2 * j + 1]) for j in range(ATT_HEADS // 2)]
    o_ref[...] = jnp.concatenate(pairs, axis=1).astype(BF16)


def _attention(q, k, v):
    tq = ATT_Q_TILE
    wide = MLA_HEADS * HEAD_PAD
    q = q.reshape(BATCH, SEQ, wide)
    k = k.reshape(BATCH, SEQ, wide)
    v = v.reshape(BATCH, SEQ, MLA_OUT)
    out = pl.pallas_call(
        _attn_kernel,
        grid=(BATCH, MLA_HEADS // ATT_HEADS, SEQ // tq),
        in_specs=[
            pl.BlockSpec((None, tq, ATT_HEADS * HEAD_PAD), lambda b, h, i: (b, i, h)),
            pl.BlockSpec((None, SEQ, ATT_HEADS * HEAD_PAD), lambda b, h, i: (b, 0, h)),
            pl.BlockSpec((None, SEQ, ATT_HEADS * V_HEAD), lambda b, h, i: (b, 0, h)),
        ],
        out_specs=pl.BlockSpec((None, tq, ATT_HEADS * V_HEAD), lambda b, h, i: (b, i, h)),
        out_shape=jax.ShapeDtypeStruct((BATCH, SEQ, MLA_OUT), BF16),
        compiler_params=_params("arbitrary", "arbitrary", "arbitrary"),
        name="mla_attention",
    )(q, k, v)
    return out.reshape(TOKENS, MLA_OUT)


def _fnet_kernel(tc_ref, ts_ref, c0_ref, s0_ref, a_ref, b_ref, o_ref, acc_ref):
    kk = pl.program_id(1)

    @pl.when(kk == 0)
    def _():
        acc_ref[...] = jnp.zeros_like(acc_ref)

    tc, ts, c0, s0 = tc_ref[...], ts_ref[...], c0_ref[...], s0_ref[...]
    cs = (c0 * tc - s0 * ts).astype(BF16)
    sn = (-s0 * tc - c0 * ts).astype(BF16)
    acc_ref[...] += _dot(cs, a_ref[...]) + _dot(sn, b_ref[...])

    @pl.when(kk == pl.num_programs(1) - 1)
    def _():
        o_ref[...] = acc_ref[...].astype(BF16)


def _dft_tables(tk):
    def cos_sin(k, m):
        ang = (2.0 * np.pi / SEQ) * ((k * m) % SEQ).astype(F32)
        return jnp.cos(ang), jnp.sin(ang)

    r = int(math.isqrt(SEQ))
    kr = jnp.arange(r, dtype=jnp.int32)[:, None]
    d = jnp.arange(tk, dtype=jnp.int32)[None, :]
    c1, s1 = cos_sin(kr * r, d)
    c2, s2 = cos_sin(kr, d)
    tc = (c1[:, None, :] * c2[None, :, :] - s1[:, None, :] * s2[None, :, :]).reshape(SEQ, tk)
    ts = (s1[:, None, :] * c2[None, :, :] + c1[:, None, :] * s2[None, :, :]).reshape(SEQ, tk)
    k = jnp.arange(SEQ, dtype=jnp.int32)
    c0, s0 = cos_sin(k[None, :], (jnp.arange(SEQ // tk, dtype=jnp.int32) * tk)[:, None])
    return tc, ts, c0[:, :, None], s0[:, :, None]


def _fnet(a_t, b_t):
    tm, tk = FNET_M_TILE, FNET_K_TILE
    n = BATCH * FNET_WIDTH
    tc, ts, c0, s0 = _dft_tables(tk)
    return pl.pallas_call(
        _fnet_kernel,
        grid=(SEQ // tm, SEQ // tk),
        in_specs=[
            pl.BlockSpec((tm, tk), lambda i, kk: (i, 0)),
            pl.BlockSpec((tm, tk), lambda i, kk: (i, 0)),
            pl.BlockSpec((None, tm, 1), lambda i, kk: (kk, i, 0)),
            pl.BlockSpec((None, tm, 1), lambda i, kk: (kk, i, 0)),
            pl.BlockSpec((tk, n), lambda i, kk: (kk, 0)),
            pl.BlockSpec((tk, n), lambda i, kk: (kk, 0)),
        ],
        out_specs=pl.BlockSpec((tm, n), lambda i, kk: (i, 0)),
        out_shape=jax.ShapeDtypeStruct((SEQ, n), BF16),
        scratch_shapes=[pltpu.VMEM((tm, n), F32)],
        compiler_params=_params("arbitrary", "arbitrary"),
        name="fnet_seq_dft",
    )(tc, ts, c0, s0, a_t, b_t)


def _top4(logits):
    rows = logits.shape[0]
    lane = lax.broadcasted_iota(jnp.int32, (rows, LANES), 1).astype(F32)
    work = logits
    vals, idxs = [], []
    for _ in range(TOP_K):
        m = jnp.max(work, axis=-1, keepdims=True)
        idx = jnp.min(jnp.where(work == m, lane, float(LANES)), axis=-1, keepdims=True)
        vals.append(m)
        idxs.append(idx.astype(jnp.int32))
        work = jnp.where(lane == idx, -jnp.inf, work)
    es = [jnp.exp(v - vals[0]) for v in vals]
    den = es[0] + es[1] + es[2] + es[3]
    lane8 = lax.broadcasted_iota(jnp.int32, (rows, 8), 1)
    top_e = jnp.zeros((rows, 8), jnp.int32)
    gates = jnp.zeros((rows, 8), F32)
    for kk in range(TOP_K):
        top_e = jnp.where(lane8 == kk, idxs[kk], top_e)
        gates = jnp.where(lane8 == kk, es[kk] / den, gates)
    return top_e, gates


def _mixer_tail(o, x_ref, gm_ref, scf_ref, shf_ref, lng_ref, lnb_ref, wr_ref, br_ref,
                xo_ref, h_ref, te_ref, gt_ref):
    xn = _ln(DEEPNORM_ALPHA * x_ref[...] + gm_ref[...] * o, LN_EPS) * lng_ref[...] + lnb_ref[...]
    xo_ref[...] = xn
    h = _ln(xn, MOD_EPS) * (1.0 + scf_ref[...]) + shf_ref[...]
    _store_row_tiles(h_ref, 0, h)
    logits = _dot(h, wr_ref[...]) + br_ref[...]
    top_e, gates = _top4(logits)
    te_ref[...] = top_e
    gt_ref[...] = gates


def _post0_kernel(oa_ref, ob_ref, wa_ref, wb_ref, *rest):
    o = _dot(oa_ref[...], wa_ref[...]) + _dot(ob_ref[...], wb_ref[...])
    _mixer_tail(o, *rest)


def _post1_kernel(u_ref, up_ref, un_ref, gb_ref, cw_ref, wo_ref, *rest):
    tm = u_ref.shape[0]
    i = pl.program_id(0)
    s0 = (i * tm) % SEQ
    u = u_ref[...].astype(F32)
    prev_row = jnp.where(s0 > 0, up_ref[...].astype(F32)[15:16, :], 0.0)
    next_row = jnp.where(s0 + tm < SEQ, un_ref[...].astype(F32)[0:1, :], 0.0)
    row = lax.broadcasted_iota(jnp.int32, u.shape, 0)
    u_m1 = jnp.where(row == 0, prev_row, pltpu.roll(u, 1, 0))
    u_p1 = jnp.where(row == tm - 1, next_row, pltpu.roll(u, tm - 1, 0))
    cw = cw_ref[...]
    y = u_m1 * cw[0:1, :] + u * cw[1:2, :] + u_p1 * cw[2:3, :]
    g = gb_ref[...].astype(F32) * y
    o = _dot(g.astype(BF16), wo_ref[...])
    _mixer_tail(o, *rest)


def _tail_specs(tm):
    row = lambda w: pl.BlockSpec((tm, w), lambda i: (i, 0))
    full = lambda shape: pl.BlockSpec(shape, lambda i: (0,) * len(shape))
    in_specs = [row(D_MODEL), _mod_spec(2, tm), _mod_spec(4, tm), _mod_spec(3, tm),
                full((1, D_MODEL)), full((1, D_MODEL)), full((D_MODEL, LANES)), full((1, LANES))]
    out_specs = [row(D_MODEL), pl.BlockSpec((tm * ROW_CHUNKS, LANES), lambda i: (i, 0)),
                 row(8), row(8)]
    out_shape = [jax.ShapeDtypeStruct((TOKENS, D_MODEL), F32),
                 jax.ShapeDtypeStruct((TOKENS * ROW_CHUNKS, LANES), F32),
                 jax.ShapeDtypeStruct((TOKENS, 8), jnp.int32),
                 jax.ShapeDtypeStruct((TOKENS, 8), F32)]
    return in_specs, out_specs, out_shape


def _tail_args(x, modl, ln_g, ln_b, w_router, b_router):
    wr = jnp.pad(w_router, ((0, 0), (0, LANES - N_EXPERTS)))
    br = jnp.concatenate([b_router, jnp.full((LANES - N_EXPERTS,), NEG_BIG, F32)]).reshape(1, LANES)
    return (x, modl, modl, modl, ln_g.reshape(1, -1), ln_b.reshape(1, -1), wr, br)


def _post0(o_a, o_b, w_out, x, modl, ln_g, ln_b, w_router, b_router):
    tm = ROW_TILE
    nst = SEQ // tm
    tin, tout, tshape = _tail_specs(tm)
    w = w_out.astype(BF16)
    full = lambda shape: pl.BlockSpec(shape, lambda i: (0,) * len(shape))
    return pl.pallas_call(
        _post0_kernel,
        grid=(TOKENS // tm,),
        in_specs=[pl.BlockSpec((tm, MLA_OUT), lambda i: (i, 0)),
                  pl.BlockSpec((tm, FNET_WIDTH), lambda i: (i % nst, i // nst)),
                  full((MLA_OUT, D_MODEL)), full((FNET_WIDTH, D_MODEL))] + tin,
        out_specs=tout,
        out_shape=tshape,
        compiler_params=_params("arbitrary"),
        name="mix_a_tail",
    )(o_a, o_b, w[:MLA_OUT], w[MLA_OUT:], *_tail_args(x, modl, ln_g, ln_b, w_router, b_router))


def _convin_kernel(x_ref, sc_ref, sh_ref, w_ref, gb_ref, u_ref):
    h = _ln(x_ref[...], MOD_EPS) * (1.0 + sc_ref[...]) + sh_ref[...]
    p = _dot(h.astype(BF16), w_ref[...])
    gb_ref[...] = p[:, :D_MODEL].astype(BF16)
    u_ref[...] = (p[:, D_MODEL:2 * D_MODEL] * p[:, 2 * D_MODEL:]).astype(BF16)


def _convin(x, modl, w_in):
    tm = ROW_TILE
    row = pl.BlockSpec((tm, D_MODEL), lambda i: (i, 0))
    return pl.pallas_call(
        _convin_kernel,
        grid=(TOKENS // tm,),
        in_specs=[row, _mod_spec(1, tm), _mod_spec(0, tm),
                  pl.BlockSpec((D_MODEL, 3 * D_MODEL), lambda i: (0, 0))],
        out_specs=[row, row],
        out_shape=[jax.ShapeDtypeStruct((TOKENS, D_MODEL), BF16)] * 2,
        compiler_params=_params("arbitrary"),
        name="conv_front",
    )(x, modl, modl, w_in.astype(BF16))


def _post1(u, gate_b, conv_w, w_out, x, modl, ln_g, ln_b, w_router, b_router):
    tm = ROW_TILE
    halo = 16
    tin, tout, tshape = _tail_specs(tm)
    nh = TOKENS // halo
    cw = jnp.zeros((8, D_MODEL), F32).at[:CONV_WIDTH].set(conv_w)
    row = pl.BlockSpec((tm, D_MODEL), lambda i: (i, 0))
    return pl.pallas_call(
        _post1_kernel,
        grid=(TOKENS // tm,),
        in_specs=[row,
                  pl.BlockSpec((halo, D_MODEL), lambda i: (jnp.maximum(i * (tm // halo) - 1, 0), 0)),
                  pl.BlockSpec((halo, D_MODEL),
                               lambda i: (jnp.minimum((i + 1) * (tm // halo), nh - 1), 0)),
                  row,
                  pl.BlockSpec((8, D_MODEL), lambda i: (0, 0)),
                  pl.BlockSpec((D_MODEL, D_MODEL), lambda i: (0, 0))] + tin,
        out_specs=tout,
        out_shape=tshape,
        compiler_params=_params("arbitrary"),
        name="conv_tail",
    )(u, u, u, gate_b, cw, w_out.astype(BF16),
      *_tail_args(x, modl, ln_g, ln_b, w_router, b_router))


def _lane_cumsum(v):
    lane = lax.broadcasted_iota(jnp.int32, v.shape, 1)
    s = 1
    while s < N_EXPERTS:
        v = v + jnp.where(lane >= s, pltpu.roll(v, s, 1), 0.0)
        s *= 2
    return v


def _rank_kernel(te_ref, dest_ref, meta_ref, cnt_ref, carry_ref, pstart_ref):
    ph = pl.program_id(0)
    i = pl.program_id(1)
    tb = te_ref.shape[0]
    lane = lax.broadcasted_iota(jnp.int32, (tb, LANES), 1)
    te = te_ref[...]
    onehot = [lane == te[:, kk:kk + 1] for kk in range(TOP_K)]
    msum = sum(oh.astype(F32) for oh in onehot)
    colsum = jnp.sum(msum, axis=0, keepdims=True)

    @pl.when((ph == 0) & (i == 0))
    def _():
        cnt_ref[...] = jnp.zeros_like(cnt_ref)

    @pl.when(ph == 0)
    def _():
        cnt_ref[...] += jnp.broadcast_to(colsum, cnt_ref.shape)

    @pl.when((ph == 1) & (i == 0))
    def _():
        cnt = cnt_ref[...]
        padded = jnp.floor((cnt + (MOE_BLOCK - 1)) * (1.0 / MOE_BLOCK)) * MOE_BLOCK
        pend = _lane_cumsum(padded)
        pstart_ref[...] = pend - padded
        carry_ref[...] = jnp.zeros_like(carry_ref)
        lane8 = lax.broadcasted_iota(jnp.int32, (8, LANES), 1)
        row8 = lax.broadcasted_iota(jnp.int32, (8, LANES), 0)
        thr = ((row8 * LANES + lane8) * MOE_BLOCK).astype(F32)
        blk = jnp.zeros((8, LANES), F32)
        for e in range(N_EXPERTS):
            pe = jnp.sum(jnp.where(lane8 == e, pend, 0.0), axis=1, keepdims=True)
            blk = blk + (pe <= thr).astype(F32)
        blk = jnp.minimum(blk, N_EXPERTS - 1.0)
        total = jnp.sum(jnp.where(lane8 == N_EXPERTS - 1, pend, 0.0), axis=1, keepdims=True)
        nact = jnp.broadcast_to(total * (1.0 / MOE_BLOCK), (8, LANES))
        info = jnp.where(row8 == 0, cnt, jnp.where(row8 == 1, pend, nact))
        meta_ref[0:8, :] = blk.astype(jnp.int32)
        meta_ref[8:16, :] = info.astype(jnp.int32)

    @pl.when(ph == 1)
    def _():
        r = lax.broadcasted_iota(jnp.int32, (tb, tb), 0)
        c = lax.broadcasted_iota(jnp.int32, (tb, tb), 1)
        lower = (c < r).astype(BF16)
        prefix = _dot(lower, msum.astype(BF16))
        base = prefix + carry_ref[0:1, :] + (pstart_ref[0:1, :] + float(MOE_BLOCK))
        lane8 = lax.broadcasted_iota(jnp.int32, (tb, 8), 1)
        dest = jnp.zeros((tb, 8), jnp.int32)
        for kk in range(TOP_K):
            dk = jnp.sum(jnp.where(onehot[kk], base, 0.0), axis=1, keepdims=True)
            dest = jnp.where(lane8 == kk, dk.astype(jnp.int32), dest)
        dest_ref[...] = dest
        carry_ref[...] += jnp.broadcast_to(colsum, carry_ref.shape)


def _rank(top_e):
    tb = RANK_TILE
    dest, meta = pl.pallas_call(
        _rank_kernel,
        grid=(2, TOKENS // tb),
        in_specs=[pl.BlockSpec((tb, 8), lambda ph, i: (i, 0))],
        out_specs=[pl.BlockSpec((tb, 8), lambda ph, i: (i * ph, 0)),
                   pl.BlockSpec((16, LANES), lambda ph, i: (0, 0))],
        out_shape=[jax.ShapeDtypeStruct((TOKENS, 8), jnp.int32),
                   jax.ShapeDtypeStruct((16, LANES), jnp.int32)],
        scratch_shapes=[pltpu.VMEM((8, LANES), F32)] * 3,
        compiler_params=_params("arbitrary", "arbitrary"),
        name="moe_rank",
    )(top_e)
    dest_flat = dest[:, :TOP_K].reshape(-1)
    block_e = meta[0:8].reshape(-1)[:MOE_NBLOCKS]
    counts = meta[8, :N_EXPERTS]
    pend = meta[9, :N_EXPERTS]
    nact = meta[10, 0:1]
    return dest_flat, block_e, counts, pend, nact


def _slot_kernel(pend_ref, cnt_ref, nact_ref, dest_ref, slot_ref):
    i = pl.program_id(0)
    tc = dest_ref.shape[0] // TOP_K

    def fill_pad(s, carry):
        slot_ref[s + MOE_BLOCK] = TOP_K * TOKENS + (s & (2 * MOE_BLOCK - 1))
        return carry

    @pl.when(i == 0)
    def _():
        lax.fori_loop(-MOE_BLOCK, 0, fill_pad, 0)
        for e in range(N_EXPERTS):
            lax.fori_loop((pend_ref[e - 1] if e else 0) + cnt_ref[e], pend_ref[e], fill_pad, 0)
        lax.fori_loop(nact_ref[0] * MOE_BLOCK, MOE_ROWS, fill_pad, 0)

    t0 = i * tc

    def scatter(r, carry):
        for kk in range(TOP_K):
            slot_ref[dest_ref[r * TOP_K + kk]] = kk * TOKENS + t0 + r
        return carry

    lax.fori_loop(0, tc, scatter, 0, unroll=8)


def _slot_table(dest_flat, counts, pend, nact):
    tc = 4096
    return pl.pallas_call(
        _slot_kernel,
        grid_spec=pltpu.PrefetchScalarGridSpec(
            num_scalar_prefetch=3,
            grid=(TOKENS // tc,),
            in_specs=[pl.BlockSpec((tc * TOP_K,), lambda i, *_: (i,), memory_space=pltpu.SMEM)],
            out_specs=pl.BlockSpec(memory_space=pltpu.SMEM),
        ),
        out_shape=jax.ShapeDtypeStruct((MOE_ROWS + MOE_BLOCK,), jnp.int32),
        compiler_params=_params("arbitrary"),
        name="moe_slots",
    )(pend, counts, nact, dest_flat)


def _moe_kernel(layer, be_ref, nact_ref, pend_ref, slot_ref,
                h_ref, wg_hbm, wl_hbm, wd_hbm, bg0, bl0, bd0, bg1, bl1, bd1, o4_ref,
                rbuf, wstage, wbf, gsem, ssem, wsem):
    step = pl.program_id(0)
    nact = nact_ref[0]
    rows = MOE_BLOCK * ROW_CHUNKS
    weights = (wg_hbm, wl_hbm, wd_hbm)
    zero = be_ref[0] >> 16
    xoff = (0, rows)
    yoff = (2 * rows, 3 * rows)
    aoff = 4 * rows + zero

    def region(off):
        return rbuf.at[pl.ds(pl.multiple_of(off, ROW_CHUNKS), rows)]

    def tile_at(buf, first):
        return buf.at[pl.ds(pl.multiple_of(first, ROW_CHUNKS), ROW_CHUNKS)]

    def gather_row(blk, r, q):
        tok = slot_ref[(blk + 1) * MOE_BLOCK + r] & (TOKENS - 1)
        return pltpu.make_async_copy(tile_at(h_ref, tok * ROW_CHUNKS),
                                     tile_at(rbuf, xoff[q] + r * ROW_CHUNKS), gsem.at[q])

    def scatter_row(blk, r, q):
        row = slot_ref[(blk + 1) * MOE_BLOCK + r]
        return pltpu.make_async_copy(tile_at(rbuf, yoff[q] + r * ROW_CHUNKS),
                                     tile_at(o4_ref, row * ROW_CHUNKS), ssem.at[q])

    def gather_wait(q):
        pltpu.make_async_copy(h_ref.at[pl.ds(0, rows)], region(xoff[q]), gsem.at[q]).wait()

    def scatter_wait(q):
        pltpu.make_async_copy(region(yoff[q]), o4_ref.at[pl.ds(0, rows)], ssem.at[q]).wait()

    def weights_start(e):
        for i, w in enumerate(weights):
            pltpu.make_async_copy(w.at[layer, e], wstage.at[i], wsem.at[i]).start(priority=1)

    def weights_wait():
        for i, w in enumerate(weights):
            pltpu.make_async_copy(w.at[layer, 0], wstage.at[i], wsem.at[i]).wait()

    @pl.when(step == 0)
    def _():
        for q in range(2):
            rbuf[pl.ds((2 + q) * rows, rows), :] = jnp.zeros((rows, LANES), F32)
            dump = o4_ref.at[pl.ds(TOP_K * TOKENS * ROW_CHUNKS + q * rows, rows)]
            cp = pltpu.make_async_copy(region(yoff[q]), dump, ssem.at[q])
            cp.start()
            cp.wait()

        def first_rows(r, carry):
            gather_row(0, r, 0).start()
            return carry

        lax.fori_loop(0, MOE_BLOCK, first_rows, 0, unroll=8)
        weights_start(be_ref[0])

    def run_block(b, q, bg_ref, bl_ref, bd_ref):
        o = 1 - q

        @pl.when(b < nact)
        def _():
            e = be_ref[b]

            @pl.when((b == 0) | (e != be_ref[jnp.maximum(b - 1, 0)]))
            def _():
                weights_wait()
                for i in range(3):
                    wbf[i] = wstage[i].astype(BF16)
                nxt = pend_ref[e] // MOE_BLOCK

                @pl.when(nxt < nact)
                def _():
                    weights_start(be_ref[jnp.minimum(nxt, MOE_NBLOCKS - 1)])

            gather_wait(q)

            @pl.when(b >= 1)
            def _():
                scatter_wait(q)

            x = _load_row_tiles(rbuf, xoff[q] + zero, MOE_BLOCK).astype(BF16)
            nb = jnp.minimum(b + 1, nact - 1)
            for r in range(MOE_BLOCK):
                gather_row(nb, r, o).start()
            for r in range(MOE_BLOCK):
                scatter_row(b - 1, r, o).start(priority=r % 2)

            g = jnp.minimum(_dot(x, wbf[0]) + bg_ref[...], SWIGLU_LIMIT)
            lin = jnp.clip(_dot(x, wbf[1]) + bl_ref[...], -SWIGLU_LIMIT, SWIGLU_LIMIT)
            a = g * jax.nn.sigmoid(SWIGLU_ALPHA * g) * (lin + 1.0)
            for ch in range(ROW_CHUNKS):
                rbuf[pl.ds(aoff + ch * MOE_BLOCK, MOE_BLOCK), :] = a[:, ch * LANES:(ch + 1) * LANES]
            a = jnp.concatenate([rbuf[pl.ds(aoff + ch * MOE_BLOCK, MOE_BLOCK), :]
                                 for ch in range(ROW_CHUNKS)], axis=1)
            _store_row_tiles(rbuf, yoff[q] + zero, _dot(a.astype(BF16), wbf[2]) + bd_ref[...])

            @pl.when(b == nact - 1)
            def _():
                def last_rows(r, carry):
                    scatter_row(b, r, q).start()
                    return carry

                lax.fori_loop(0, MOE_BLOCK, last_rows, 0, unroll=8)
                gather_wait(o)
                scatter_wait(o)
                scatter_wait(q)

    run_block(2 * step, 0, bg0, bl0, bd0)
    run_block(2 * step + 1, 1, bg1, bl1, bd1)


def _moe_experts(l, h_tiles, block_e, nact, pend, slot_row, w_glu, b_glu, w_lin, b_lin, w_down, b_down):
    rows = MOE_BLOCK * ROW_CHUNKS

    def bspec(q):
        return pl.BlockSpec((None, None, 1, D_MODEL),
                            lambda s, be, na, pe, sl: (l, be[jnp.minimum(2 * s + q, na[0] - 1)], 0, 0))

    b3 = lambda b: b.reshape(DEPTH, N_EXPERTS, 1, D_MODEL)
    hbm = pl.BlockSpec(memory_space=pl.ANY)
    biases = (b3(b_glu), b3(b_lin), b3(b_down))
    return pl.pallas_call(
        functools.partial(_moe_kernel, l),
        grid_spec=pltpu.PrefetchScalarGridSpec(
            num_scalar_prefetch=4,
            grid=(MOE_NBLOCKS // 2,),
            in_specs=[hbm, hbm, hbm, hbm] + [bspec(0)] * 3 + [bspec(1)] * 3,
            out_specs=hbm,
            scratch_shapes=[
                pltpu.VMEM((5 * rows, LANES), F32),
                pltpu.VMEM((3, D_MODEL, D_MODEL), F32),
                pltpu.VMEM((3, D_MODEL, D_MODEL), BF16),
                pltpu.SemaphoreType.DMA((2,)), pltpu.SemaphoreType.DMA((2,)),
                pltpu.SemaphoreType.DMA((3,))],
        ),
        out_shape=jax.ShapeDtypeStruct(((TOP_K * TOKENS + 2 * MOE_BLOCK) * ROW_CHUNKS, LANES), F32),
        compiler_params=_params("arbitrary"),
        name="moe_experts",
    )(block_e, nact, pend, slot_row, h_tiles, w_glu, w_lin, w_down, *biases, *biases)


def _combine_kernel(y0_ref, y1_ref, y2_ref, y3_ref, gt_ref, x_ref, gf_ref, lng_ref, lnb_ref, xo_ref):
    tm = x_ref.shape[0]
    gt = gt_ref[...]
    o = gt[:, 0:1] * _load_row_tiles(y0_ref, 0, tm)
    for kk, y_ref in enumerate((y1_ref, y2_ref, y3_ref), start=1):
        o = o + gt[:, kk:kk + 1] * _load_row_tiles(y_ref, 0, tm)
    xo_ref[...] = (_ln(DEEPNORM_ALPHA * x_ref[...] + gf_ref[...] * o, LN_EPS) * lng_ref[...]
                   + lnb_ref[...])


def _combine(o4, gates, x, modl, ln_g, ln_b):
    tm = ROW_TILE
    row = pl.BlockSpec((tm, D_MODEL), lambda i: (i, 0))
    vec = pl.BlockSpec((1, D_MODEL), lambda i: (0, 0))
    nt = TOKENS // tm
    yspec = lambda kk: pl.BlockSpec((tm * ROW_CHUNKS, LANES), lambda i: (kk * nt + i, 0))
    return pl.pallas_call(
        _combine_kernel,
        grid=(nt,),
        in_specs=[yspec(0), yspec(1), yspec(2), yspec(3),
                  pl.BlockSpec((tm, 8), lambda i: (i, 0)),
                  row, _mod_spec(5, tm), vec, vec],
        out_specs=row,
        out_shape=jax.ShapeDtypeStruct((TOKENS, D_MODEL), F32),
        compiler_params=_params("arbitrary"),
        name="moe_combine",
    )(o4, o4, o4, o4, gates, x, modl, ln_g.reshape(1, -1), ln_b.reshape(1, -1))


def _rope_cos_sin(positions):
    inv_freq = ROPE_THETA ** (-jnp.arange(0, QK_ROPE, 2, dtype=F32) / QK_ROPE)
    ang = positions.astype(F32)[..., None] * inv_freq
    return jnp.concatenate([jnp.cos(ang), jnp.sin(ang)], -1).reshape(TOKENS, QK_ROPE)


def _moe_layer(l, x, h, top_e, gates, modl, ln_g, ln_b, w_glu, b_glu, w_lin, b_lin, w_down, b_down):
    dest_flat, block_e, counts, pend, nact = _rank(top_e)
    slot_row = _slot_table(dest_flat, counts, pend, nact)
    o4 = _moe_experts(l, h, block_e, nact, pend, slot_row, w_glu, b_glu, w_lin, b_lin, w_down, b_down)
    return _combine(o4, gates, x, modl, ln_g[l], ln_b[l])


def kernel(x, c, positions, ada_w, ada_b, ln_mix_g, ln_mix_b, ln_ffn_g, ln_ffn_b, mla_w_in, mla_q_norm_g, mla_kv_norm_g, mla_w_uq, mla_w_uk, mla_w_uv, mix_a_w_out, conv_w_in, conv_w, conv_w_out, moe_w_router, moe_b_router, moe_w_glu, moe_b_glu, moe_w_lin, moe_b_lin, moe_w_down, moe_b_down):
    mod = _modulation(c, ada_w, ada_b)
    rope_cs = _rope_cos_sin(positions)
    x = x.reshape(TOKENS, D_MODEL)
    for l in range(DEPTH):
        modl = mod[l]
        i = l // 2
        if l % 2 == 0:
            q, k, v, a_t, b_t = _pre0(x, modl, rope_cs, mla_w_in[i], mla_q_norm_g[i],
                                      mla_kv_norm_g[i], mla_w_uq[i], mla_w_uk[i], mla_w_uv[i])
            o_a = _attention(q, k, v)
            o_b = _fnet(a_t, b_t)
            x, h, top_e, gates = _post0(o_a, o_b, mix_a_w_out[i], x, modl, ln_mix_g[l], ln_mix_b[l],
                                        moe_w_router[l], moe_b_router[l])
        else:
            gate_b, u = _convin(x, modl, conv_w_in[i])
            x, h, top_e, gates = _post1(u, gate_b, conv_w[i], conv_w_out[i], x, modl, ln_mix_g[l],
                                        ln_mix_b[l], moe_w_router[l], moe_b_router[l])
        x = _moe_layer(l, x, h, top_e, gates, modl, ln_ffn_g, ln_ffn_b, moe_w_glu, moe_b_glu,
                       moe_w_lin, moe_b_lin, moe_w_down, moe_b_down)
    return x.reshape(BATCH, SEQ, D_MODEL)
```

```python
import functools
import math

import numpy as np
import jax
import jax.numpy as jnp
from jax import lax
from jax.experimental import pallas as pl
from jax.experimental.pallas import tpu as pltpu

D_MODEL = 1024
BATCH = 4
SEQ = 4096
DEPTH = 2
TOKENS = BATCH * SEQ

MLA_HEADS = 8
QK_NOPE = 64
QK_ROPE = 32
V_HEAD = 64
Q_RANK = 256
KV_RANK = 128
ROPE_THETA = 10000.0
MLA_OUT = MLA_HEADS * V_HEAD
FNET_GROUPS = 8
FNET_GROUP_DIM = 64
FNET_WIDTH = FNET_GROUPS * FNET_GROUP_DIM
CONV_WIDTH = 3
N_EXPERTS = 32
TOP_K = 4
SWIGLU_LIMIT = 7.0
SWIGLU_ALPHA = 1.702
DEEPNORM_ALPHA = (2 * DEPTH) ** 0.25
LN_EPS = 1e-5
MOD_EPS = 1e-6
RMS_EPS = 1e-6
QK_SCALE = (QK_NOPE + QK_ROPE) ** -0.5
LOG2E = math.log2(math.e)

LANES = 128
HEAD_PAD = 128
MODEL_CHUNKS = D_MODEL // LANES
ROW_CHUNKS = MODEL_CHUNKS // 2
VMEM_LIMIT = 56 * 1024 * 1024

ROW_TILE = 256
ATT_Q_TILE = 256
ATT_HEADS = 8
FNET_M_TILE = 1024
FNET_K_TILE = 512
RANK_TILE = 1024
MOE_BLOCK = 256
MOE_ROWS = TOKENS * TOP_K + N_EXPERTS * MOE_BLOCK
MOE_NBLOCKS = MOE_ROWS // MOE_BLOCK
NEG_BIG = -1e30

BF16 = jnp.bfloat16
F32 = jnp.float32
U32 = jnp.uint32


def _params(*sem):
    return pltpu.CompilerParams(dimension_semantics=sem, vmem_limit_bytes=VMEM_LIMIT)


def _ln(v, eps):
    mu = jnp.mean(v, axis=-1, keepdims=True)
    d = v - mu
    var = jnp.mean(d * d, axis=-1, keepdims=True)
    return d * lax.rsqrt(var + eps)


def _rms(v, g):
    return v * lax.rsqrt(jnp.mean(v * v, axis=-1, keepdims=True) + RMS_EPS) * g


def _dot(a, b):
    return jnp.dot(a, b, preferred_element_type=F32)


def _bf16_bits(v):
    return lax.bitcast_convert_type(v.astype(BF16).astype(F32), U32)


def _store_row_tiles(ref, base, val):
    rows = val.shape[0]
    half = D_MODEL // 2
    for ch in range(ROW_CHUNKS):
        lo = _bf16_bits(val[:, ch * LANES:(ch + 1) * LANES])
        hi = _bf16_bits(val[:, half + ch * LANES:half + (ch + 1) * LANES])
        ref[pl.ds(base + ch, rows, stride=ROW_CHUNKS), :] = lax.shift_right_logical(lo, U32(16)) | hi


def _load_row_tiles(ref, base, rows):
    words = [ref[pl.ds(base + ch, rows, stride=ROW_CHUNKS), :] for ch in range(ROW_CHUNKS)]
    lo = [lax.bitcast_convert_type(lax.shift_left(w, U32(16)), F32) for w in words]
    hi = [lax.bitcast_convert_type(w & U32(0xFFFF0000), F32) for w in words]
    return jnp.concatenate(lo + hi, axis=1)


def _mod_kernel(c_ref, w_ref, b_ref, o_ref):
    c = c_ref[...]
    cond = c * jax.nn.sigmoid(c)
    o_ref[...] = _dot(cond, w_ref[...]) + b_ref[...]


def _modulation(c, ada_w, ada_b):
    tn = 1536
    n = 6 * D_MODEL
    c8 = jnp.zeros((8, D_MODEL), F32).at[:BATCH].set(c)
    out = pl.pallas_call(
        _mod_kernel,
        grid=(DEPTH, n // tn),
        in_specs=[
            pl.BlockSpec((8, D_MODEL), lambda l, j: (0, 0)),
            pl.BlockSpec((None, D_MODEL, tn), lambda l, j: (l, 0, j)),
            pl.BlockSpec((None, 1, tn), lambda l, j: (l, 0, j)),
        ],
        out_specs=pl.BlockSpec((None, 8, tn), lambda l, j: (l, 0, j)),
        out_shape=jax.ShapeDtypeStruct((DEPTH, 8, n), F32),
        compiler_params=_params("arbitrary", "arbitrary"),
        name="adaln_mod",
    )(c8, ada_w, ada_b.reshape(DEPTH, 1, n))
    return out[:, :BATCH].reshape(DEPTH, BATCH, 6, 1, D_MODEL)


def _mod_spec(chunk, tile):
    return pl.BlockSpec((None, None, 1, D_MODEL),
                        lambda i: ((i * tile) // SEQ, chunk, 0, 0))


def _rope(t, tab):
    c, s1, s2 = tab[:, :LANES], tab[:, LANES:2 * LANES], tab[:, 2 * LANES:]
    outs = []
    for h in range(MLA_HEADS):
        th = t[:, h * HEAD_PAD:(h + 1) * HEAD_PAD]
        outs.append(th * c + pltpu.roll(th, HEAD_PAD - QK_ROPE // 2, 1) * s1
                    + pltpu.roll(th, QK_ROPE // 2, 1) * s2)
    return jnp.concatenate(outs, axis=1)


def _pre0_kernel(x_ref, sc_ref, sh_ref, cs_ref, place_ref, bias_ref, win_ref, gq_ref, gkv_ref,
                 wuq_ref, wk_ref, wuv_ref, dft_ref, q_ref, k_ref, v_ref, a_ref, b_ref):
    h = _ln(x_ref[...], MOD_EPS) * (1.0 + sc_ref[...]) + sh_ref[...]
    p = _dot(h.astype(BF16), win_ref[...])
    c_q = p[:, :Q_RANK]
    c_kv = p[:, Q_RANK:Q_RANK + KV_RANK]
    u_f = p[:, Q_RANK + KV_RANK:Q_RANK + KV_RANK + FNET_WIDTH]
    k_r = p[:, Q_RANK + KV_RANK + FNET_WIDTH:]
    cs = cs_ref[...]
    cs_hi = cs.astype(BF16)
    cs_lo = (cs - cs_hi.astype(F32)).astype(BF16)
    tab = _dot(cs_hi, place_ref[...]) + _dot(cs_lo, place_ref[...]) + bias_ref[...]
    q = _dot(_rms(c_q, gq_ref[...]).astype(BF16), wuq_ref[...]) * (QK_SCALE * LOG2E)
    q_ref[...] = _rope(q, tab).astype(BF16)
    ckv = _rms(c_kv, gkv_ref[...]).astype(BF16)
    kin = jnp.concatenate([ckv, k_r.astype(BF16)], axis=1)
    k_ref[...] = _rope(_dot(kin, wk_ref[...]), tab).astype(BF16)
    v_ref[...] = _dot(ckv, wuv_ref[...]).astype(BF16)
    ab = _dot(u_f.astype(BF16), dft_ref[...])
    a_ref[...] = ab[:, :FNET_WIDTH].astype(BF16)
    b_ref[...] = ab[:, FNET_WIDTH:].astype(BF16)


def _channel_dft():
    n = np.arange(FNET_GROUP_DIM)
    ang = 2.0 * np.pi * ((n[:, None] * n[None, :]) % FNET_GROUP_DIM) / FNET_GROUP_DIM
    norm = 1.0 / math.sqrt(SEQ * FNET_GROUP_DIM)
    eye = np.eye(FNET_GROUPS)
    cc = np.kron(eye, np.cos(ang) * norm)
    ss = np.kron(eye, np.sin(ang) * norm)
    return jnp.asarray(np.concatenate([cc, ss], axis=1), BF16)


def _rope_placement():
    half = QK_ROPE // 2
    place = np.zeros((QK_ROPE, 3 * LANES), np.float32)
    bias = np.zeros((1, 3 * LANES), np.float32)
    bias[0, :QK_NOPE] = 1.0
    for j in range(half):
        place[j, QK_NOPE + j] = 1.0
        place[j, QK_NOPE + half + j] = 1.0
        place[half + j, LANES + QK_NOPE + j] = -1.0
        place[half + j, 2 * LANES + QK_NOPE + half + j] = 1.0
    return jnp.asarray(place, BF16), jnp.asarray(bias)


def _pre0(x, modl, rope_cs, w_in, gq, gkv, w_uq, w_uk, w_uv):
    tm = ROW_TILE
    nq = Q_RANK + KV_RANK
    w_in_r = jnp.concatenate(
        [w_in[:, :nq], w_in[:, nq + QK_ROPE:], w_in[:, nq:nq + QK_ROPE],
         jnp.zeros((D_MODEL, LANES - QK_ROPE), F32)], axis=1).astype(BF16)
    wuq_p = jnp.pad(w_uq.reshape(Q_RANK, MLA_HEADS, QK_NOPE + QK_ROPE),
                    ((0, 0), (0, 0), (0, HEAD_PAD - QK_NOPE - QK_ROPE)))
    wuq_p = wuq_p.reshape(Q_RANK, MLA_HEADS * HEAD_PAD).astype(BF16)
    wuk_p = jnp.pad(w_uk.reshape(KV_RANK, MLA_HEADS, QK_NOPE),
                    ((0, 0), (0, 0), (0, HEAD_PAD - QK_NOPE))).reshape(KV_RANK, MLA_HEADS * HEAD_PAD)
    place = np.zeros((LANES, MLA_HEADS, HEAD_PAD), np.float32)
    for j in range(QK_ROPE):
        place[j, :, QK_NOPE + j] = 1.0
    wk_p = jnp.concatenate([wuk_p, jnp.asarray(place.reshape(LANES, -1))], axis=0).astype(BF16)
    full = lambda shape: pl.BlockSpec(shape, lambda i: (0,) * len(shape))
    nst = SEQ // tm
    wide = MLA_HEADS * HEAD_PAD
    return pl.pallas_call(
        _pre0_kernel,
        grid=(TOKENS // tm,),
        in_specs=[
            pl.BlockSpec((tm, D_MODEL), lambda i: (i, 0)),
            _mod_spec(1, tm), _mod_spec(0, tm),
            pl.BlockSpec((tm, QK_ROPE), lambda i: (i, 0)),
            full((QK_ROPE, 3 * LANES)), full((1, 3 * LANES)),
            full((D_MODEL, D_MODEL)), full((1, Q_RANK)), full((1, KV_RANK)),
            full((Q_RANK, wide)), full((2 * LANES, wide)), full((KV_RANK, MLA_OUT)),
            full((FNET_WIDTH, 2 * FNET_WIDTH)),
        ],
        out_specs=[
            pl.BlockSpec((tm, wide), lambda i: (i, 0)),
            pl.BlockSpec((tm, wide), lambda i: (i, 0)),
            pl.BlockSpec((tm, MLA_OUT), lambda i: (i, 0)),
            pl.BlockSpec((tm, FNET_WIDTH), lambda i: (i % nst, i // nst)),
            pl.BlockSpec((tm, FNET_WIDTH), lambda i: (i % nst, i // nst)),
        ],
        out_shape=[
            jax.ShapeDtypeStruct((TOKENS, wide), BF16),
            jax.ShapeDtypeStruct((TOKENS, wide), BF16),
            jax.ShapeDtypeStruct((TOKENS, MLA_OUT), BF16),
            jax.ShapeDtypeStruct((SEQ, BATCH * FNET_WIDTH), BF16),
            jax.ShapeDtypeStruct((SEQ, BATCH * FNET_WIDTH), BF16),
        ],
        compiler_params=_params("arbitrary"),
        name="mla_fnet_front",
    )(x, modl, modl, rope_cs, *_rope_placement(), w_in_r, gq.reshape(1, -1), gkv.reshape(1, -1),
      wuq_p, wk_p, w_uv.astype(BF16), _channel_dft())


def _attn_kernel(q_ref, k_ref, v_ref, o_ref):
    outs = []
    for hh in range(ATT_HEADS):
        q = q_ref[:, hh * HEAD_PAD:(hh + 1) * HEAD_PAD]
        k = k_ref[:, hh * HEAD_PAD:(hh + 1) * HEAD_PAD]
        s = lax.dot_general(q, k, (((1,), (1,)), ((), ())), preferred_element_type=F32)
        m = jnp.max(s, axis=-1, keepdims=True)
        p = jnp.exp2(s - m)
        l = jnp.sum(p, axis=-1, keepdims=True)
        v = v_ref[:, (hh // 2) * LANES:(hh // 2 + 1) * LANES]
        outs.append(_dot(p.astype(BF16), v) / l)
    lane = lax.broadcasted_iota(jnp.int32, outs[0].shape, 1)
    pairs = [jnp.where(lane < V_HEAD, outs[2 * j], outs[2 * j + 1]) for j in range(ATT_HEADS // 2)]
    o_ref[...] = jnp.concatenate(pairs, axis=1).astype(BF16)


def _attention(q, k, v):
    tq = ATT_Q_TILE
    wide = MLA_HEADS * HEAD_PAD
    q = q.reshape(BATCH, SEQ, wide)
    k = k.reshape(BATCH, SEQ, wide)
    v = v.reshape(BATCH, SEQ, MLA_OUT)
    out = pl.pallas_call(
        _attn_kernel,
        grid=(BATCH, MLA_HEADS // ATT_HEADS, SEQ // tq),
        in_specs=[
            pl.BlockSpec((None, tq, ATT_HEADS * HEAD_PAD), lambda b, h, i: (b, i, h)),
            pl.BlockSpec((None, SEQ, ATT_HEADS * HEAD_PAD), lambda b, h, i: (b, 0, h)),
            pl.BlockSpec((None, SEQ, ATT_HEADS * V_HEAD), lambda b, h, i: (b, 0, h)),
        ],
        out_specs=pl.BlockSpec((None, tq, ATT_HEADS * V_HEAD), lambda b, h, i: (b, i, h)),
        out_shape=jax.ShapeDtypeStruct((BATCH, SEQ, MLA_OUT), BF16),
        compiler_params=_params("arbitrary", "arbitrary", "arbitrary"),
        name="mla_attention",
    )(q, k, v)
    return out.reshape(TOKENS, MLA_OUT)


def _fnet_kernel(tc_ref, ts_ref, c0_ref, s0_ref, a_ref, b_ref, o_ref, acc_ref):
    kk = pl.program_id(1)

    @pl.when(kk == 0)
    def _():
        acc_ref[...] = jnp.zeros_like(acc_ref)

    tc, ts, c0, s0 = tc_ref[...], ts_ref[...], c0_ref[...], s0_ref[...]
    cs = (c0 * tc - s0 * ts).astype(BF16)
    sn = (-s0 * tc - c0 * ts).astype(BF16)
    acc_ref[...] += _dot(cs, a_ref[...]) + _dot(sn, b_ref[...])

    @pl.when(kk == pl.num_programs(1) - 1)
    def _():
        o_ref[...] = acc_ref[...].astype(BF16)


def _dft_tables(tk):
    def cos_sin(k, m):
        ang = (2.0 * np.pi / SEQ) * ((k * m) % SEQ).astype(F32)
        return jnp.cos(ang), jnp.sin(ang)

    r = int(math.isqrt(SEQ))
    kr = jnp.arange(r, dtype=jnp.int32)[:, None]
    d = jnp.arange(tk, dtype=jnp.int32)[None, :]
    c1, s1 = cos_sin(kr * r, d)
    c2, s2 = cos_sin(kr, d)
    tc = (c1[:, None, :] * c2[None, :, :] - s1[:, None, :] * s2[None, :, :]).reshape(SEQ, tk)
    ts = (s1[:, None, :] * c2[None, :, :] + c1[:, None, :] * s2[None, :, :]).reshape(SEQ, tk)
    k = jnp.arange(SEQ, dtype=jnp.int32)
    c0, s0 = cos_sin(k[None, :], (jnp.arange(SEQ // tk, dtype=jnp.int32) * tk)[:, None])
    return tc, ts, c0[:, :, None], s0[:, :, None]


def _fnet(a_t, b_t):
    tm, tk = FNET_M_TILE, FNET_K_TILE
    n = BATCH * FNET_WIDTH
    tc, ts, c0, s0 = _dft_tables(tk)
    return pl.pallas_call(
        _fnet_kernel,
        grid=(SEQ // tm, SEQ // tk),
        in_specs=[
            pl.BlockSpec((tm, tk), lambda i, kk: (i, 0)),
            pl.BlockSpec((tm, tk), lambda i, kk: (i, 0)),
            pl.BlockSpec((None, tm, 1), lambda i, kk: (kk, i, 0)),
            pl.BlockSpec((None, tm, 1), lambda i, kk: (kk, i, 0)),
            pl.BlockSpec((tk, n), lambda i, kk: (kk, 0)),
            pl.BlockSpec((tk, n), lambda i, kk: (kk, 0)),
        ],
        out_specs=pl.BlockSpec((tm, n), lambda i, kk: (i, 0)),
        out_shape=jax.ShapeDtypeStruct((SEQ, n), BF16),
        scratch_shapes=[pltpu.VMEM((tm, n), F32)],
        compiler_params=_params("arbitrary", "arbitrary"),
        name="fnet_seq_dft",
    )(tc, ts, c0, s0, a_t, b_t)


def _top4(logits):
    rows = logits.shape[0]
    lane = lax.broadcasted_iota(jnp.int32, (rows, LANES), 1).astype(F32)
    work = logits
    vals, idxs = [], []
    for _ in range(TOP_K):
        m = jnp.max(work, axis=-1, keepdims=True)
        idx = jnp.min(jnp.where(work == m, lane, float(LANES)), axis=-1, keepdims=True)
        vals.append(m)
        idxs.append(idx.astype(jnp.int32))
        work = jnp.where(lane == idx, -jnp.inf, work)
    es = [jnp.exp(v - vals[0]) for v in vals]
    den = es[0] + es[1] + es[2] + es[3]
    lane8 = lax.broadcasted_iota(jnp.int32, (rows, 8), 1)
    top_e = jnp.zeros((rows, 8), jnp.int32)
    gates = jnp.zeros((rows, 8), F32)
    for kk in range(TOP_K):
        top_e = jnp.where(lane8 == kk, idxs[kk], top_e)
        gates = jnp.where(lane8 == kk, es[kk] / den, gates)
    return top_e, gates


def _mixer_tail(o, x_ref, gm_ref, scf_ref, shf_ref, lng_ref, lnb_ref, wr_ref, br_ref,
                xo_ref, h_ref, te_ref, gt_ref):
    xn = _ln(DEEPNORM_ALPHA * x_ref[...] + gm_ref[...] * o, LN_EPS) * lng_ref[...] + lnb_ref[...]
    xo_ref[...] = xn
    h = _ln(xn, MOD_EPS) * (1.0 + scf_ref[...]) + shf_ref[...]
    _store_row_tiles(h_ref, 0, h)
    logits = _dot(h, wr_ref[...]) + br_ref[...]
    top_e, gates = _top4(logits)
    te_ref[...] = top_e
    gt_ref[...] = gates


def _post0_kernel(oa_ref, ob_ref, wa_ref, wb_ref, *rest):
    o = _dot(oa_ref[...], wa_ref[...]) + _dot(ob_ref[...], wb_ref[...])
    _mixer_tail(o, *rest)


def _post1_kernel(u_ref, up_ref, un_ref, gb_ref, cw_ref, wo_ref, *rest):
    tm = u_ref.shape[0]
    i = pl.program_id(0)
    s0 = (i * tm) % SEQ
    u = u_ref[...].astype(F32)
    prev_row = jnp.where(s0 > 0, up_ref[...].astype(F32)[15:16, :], 0.0)
    next_row = jnp.where(s0 + tm < SEQ, un_ref[...].astype(F32)[0:1, :], 0.0)
    row = lax.broadcasted_iota(jnp.int32, u.shape, 0)
    u_m1 = jnp.where(row == 0, prev_row, pltpu.roll(u, 1, 0))
    u_p1 = jnp.where(row == tm - 1, next_row, pltpu.roll(u, tm - 1, 0))
    cw = cw_ref[...]
    y = u_m1 * cw[0:1, :] + u * cw[1:2, :] + u_p1 * cw[2:3, :]
    g = gb_ref[...].astype(F32) * y
    o = _dot(g.astype(BF16), wo_ref[...])
    _mixer_tail(o, *rest)


def _tail_specs(tm):
    row = lambda w: pl.BlockSpec((tm, w), lambda i: (i, 0))
    full = lambda shape: pl.BlockSpec(shape, lambda i: (0,) * len(shape))
    in_specs = [row(D_MODEL), _mod_spec(2, tm), _mod_spec(4, tm), _mod_spec(3, tm),
                full((1, D_MODEL)), full((1, D_MODEL)), full((D_MODEL, LANES)), full((1, LANES))]
    out_specs = [row(D_MODEL), pl.BlockSpec((tm * ROW_CHUNKS, LANES), lambda i: (i, 0)),
                 row(8), row(8)]
    out_shape = [jax.ShapeDtypeStruct((TOKENS, D_MODEL), F32),
                 jax.ShapeDtypeStruct((TOKENS * ROW_CHUNKS, LANES), U32),
                 jax.ShapeDtypeStruct((TOKENS, 8), jnp.int32),
                 jax.ShapeDtypeStruct((TOKENS, 8), F32)]
    return in_specs, out_specs, out_shape


def _tail_args(x, modl, ln_g, ln_b, w_router, b_router):
    wr = jnp.pad(w_router, ((0, 0), (0, LANES - N_EXPERTS)))
    br = jnp.concatenate([b_router, jnp.full((LANES - N_EXPERTS,), NEG_BIG, F32)]).reshape(1, LANES)
    return (x, modl, modl, modl, ln_g.reshape(1, -1), ln_b.reshape(1, -1), wr, br)


def _post0(o_a, o_b, w_out, x, modl, ln_g, ln_b, w_router, b_router):
    tm = ROW_TILE
    nst = SEQ // tm
    tin, tout, tshape = _tail_specs(tm)
    w = w_out.astype(BF16)
    full = lambda shape: pl.BlockSpec(shape, lambda i: (0,) * len(shape))
    return pl.pallas_call(
        _post0_kernel,
        grid=(TOKENS // tm,),
        in_specs=[pl.BlockSpec((tm, MLA_OUT), lambda i: (i, 0)),
                  pl.BlockSpec((tm, FNET_WIDTH), lambda i: (i % nst, i // nst)),
                  full((MLA_OUT, D_MODEL)), full((FNET_WIDTH, D_MODEL))] + tin,
        out_specs=tout,
        out_shape=tshape,
        compiler_params=_params("arbitrary"),
        name="mix_a_tail",
    )(o_a, o_b, w[:MLA_OUT], w[MLA_OUT:], *_tail_args(x, modl, ln_g, ln_b, w_router, b_router))


def _convin_kernel(x_ref, sc_ref, sh_ref, w_ref, gb_ref, u_ref):
    h = _ln(x_ref[...], MOD_EPS) * (1.0 + sc_ref[...]) + sh_ref[...]
    p = _dot(h.astype(BF16), w_ref[...])
    gb_ref[...] = p[:, :D_MODEL].astype(BF16)
    u_ref[...] = (p[:, D_MODEL:2 * D_MODEL] * p[:, 2 * D_MODEL:]).astype(BF16)


def _convin(x, modl, w_in):
    tm = ROW_TILE
    row = pl.BlockSpec((tm, D_MODEL), lambda i: (i, 0))
    return pl.pallas_call(
        _convin_kernel,
        grid=(TOKENS // tm,),
        in_specs=[row, _mod_spec(1, tm), _mod_spec(0, tm),
                  pl.BlockSpec((D_MODEL, 3 * D_MODEL), lambda i: (0, 0))],
        out_specs=[row, row],
        out_shape=[jax.ShapeDtypeStruct((TOKENS, D_MODEL), BF16)] * 2,
        compiler_params=_params("arbitrary"),
        name="conv_front",
    )(x, modl, modl, w_in.astype(BF16))


def _post1(u, gate_b, conv_w, w_out, x, modl, ln_g, ln_b, w_router, b_router):
    tm = ROW_TILE
    halo = 16
    tin, tout, tshape = _tail_specs(tm)
    nh = TOKENS // halo
    cw = jnp.zeros((8, D_MODEL), F32).at[:CONV_WIDTH].set(conv_w)
    row = pl.BlockSpec((tm, D_MODEL), lambda i: (i, 0))
    return pl.pallas_call(
        _post1_kernel,
        grid=(TOKENS // tm,),
        in_specs=[row,
                  pl.BlockSpec((halo, D_MODEL), lambda i: (jnp.maximum(i * (tm // halo) - 1, 0), 0)),
                  pl.BlockSpec((halo, D_MODEL),
                               lambda i: (jnp.minimum((i + 1) * (tm // halo), nh - 1), 0)),
                  row,
                  pl.BlockSpec((8, D_MODEL), lambda i: (0, 0)),
                  pl.BlockSpec((D_MODEL, D_MODEL), lambda i: (0, 0))] + tin,
        out_specs=tout,
        out_shape=tshape,
        compiler_params=_params("arbitrary"),
        name="conv_tail",
    )(u, u, u, gate_b, cw, w_out.astype(BF16),
      *_tail_args(x, modl, ln_g, ln_b, w_router, b_router))


def _lane_cumsum(v):
    lane = lax.broadcasted_iota(jnp.int32, v.shape, 1)
    s = 1
    while s < N_EXPERTS:
        v = v + jnp.where(lane >= s, pltpu.roll(v, s, 1), 0.0)
        s *= 2
    return v


def _rank_kernel(te_ref, dest_ref, meta_ref, cnt_ref, carry_ref, pstart_ref):
    ph = pl.program_id(0)
    i = pl.program_id(1)
    tb = te_ref.shape[0]
    lane = lax.broadcasted_iota(jnp.int32, (tb, LANES), 1)
    te = te_ref[...]
    onehot = [lane == te[:, kk:kk + 1] for kk in range(TOP_K)]
    msum = sum(oh.astype(F32) for oh in onehot)
    colsum = jnp.sum(msum, axis=0, keepdims=True)

    @pl.when((ph == 0) & (i == 0))
    def _():
        cnt_ref[...] = jnp.zeros_like(cnt_ref)

    @pl.when(ph == 0)
    def _():
        cnt_ref[...] += jnp.broadcast_to(colsum, cnt_ref.shape)

    @pl.when((ph == 1) & (i == 0))
    def _():
        cnt = cnt_ref[...]
        padded = jnp.floor((cnt + (MOE_BLOCK - 1)) * (1.0 / MOE_BLOCK)) * MOE_BLOCK
        pend = _lane_cumsum(padded)
        pstart_ref[...] = pend - padded
        carry_ref[...] = jnp.zeros_like(carry_ref)
        lane8 = lax.broadcasted_iota(jnp.int32, (8, LANES), 1)
        row8 = lax.broadcasted_iota(jnp.int32, (8, LANES), 0)
        thr = ((row8 * LANES + lane8) * MOE_BLOCK).astype(F32)
        blk = jnp.zeros((8, LANES), F32)
        for e in range(N_EXPERTS):
            pe = jnp.sum(jnp.where(lane8 == e, pend, 0.0), axis=1, keepdims=True)
            blk = blk + (pe <= thr).astype(F32)
        blk = jnp.minimum(blk, N_EXPERTS - 1.0)
        total = jnp.sum(jnp.where(lane8 == N_EXPERTS - 1, pend, 0.0), axis=1, keepdims=True)
        nact = jnp.broadcast_to(total * (1.0 / MOE_BLOCK), (8, LANES))
        info = jnp.where(row8 == 0, cnt, jnp.where(row8 == 1, pend, nact))
        meta_ref[0:8, :] = blk.astype(jnp.int32)
        meta_ref[8:16, :] = info.astype(jnp.int32)

    @pl.when(ph == 1)
    def _():
        r = lax.broadcasted_iota(jnp.int32, (tb, tb), 0)
        c = lax.broadcasted_iota(jnp.int32, (tb, tb), 1)
        lower = (c < r).astype(BF16)
        prefix = _dot(lower, msum.astype(BF16))
        base = prefix + carry_ref[0:1, :] + (pstart_ref[0:1, :] + float(MOE_BLOCK))
        lane8 = lax.broadcasted_iota(jnp.int32, (tb, 8), 1)
        dest = jnp.zeros((tb, 8), jnp.int32)
        for kk in range(TOP_K):
            dk = jnp.sum(jnp.where(onehot[kk], base, 0.0), axis=1, keepdims=True)
            dest = jnp.where(lane8 == kk, dk.astype(jnp.int32), dest)
        dest_ref[...] = dest
        carry_ref[...] += jnp.broadcast_to(colsum, carry_ref.shape)


def _rank(top_e):
    tb = RANK_TILE
    dest, meta = pl.pallas_call(
        _rank_kernel,
        grid=(2, TOKENS // tb),
        in_specs=[pl.BlockSpec((tb, 8), lambda ph, i: (i, 0))],
        out_specs=[pl.BlockSpec((tb, 8), lambda ph, i: (i * ph, 0)),
                   pl.BlockSpec((16, LANES), lambda ph, i: (0, 0))],
        out_shape=[jax.ShapeDtypeStruct((TOKENS, 8), jnp.int32),
                   jax.ShapeDtypeStruct((16, LANES), jnp.int32)],
        scratch_shapes=[pltpu.VMEM((8, LANES), F32)] * 3,
        compiler_params=_params("arbitrary", "arbitrary"),
        name="moe_rank",
    )(top_e)
    dest_flat = dest[:, :TOP_K].reshape(-1)
    block_e = meta[0:8].reshape(-1)[:MOE_NBLOCKS]
    counts = meta[8, :N_EXPERTS]
    pend = meta[9, :N_EXPERTS]
    nact = meta[10, 0:1]
    return dest_flat, block_e, counts, pend, nact


def _slot_kernel(pend_ref, cnt_ref, nact_ref, dest_ref, slot_ref):
    i = pl.program_id(0)
    tc = dest_ref.shape[0] // TOP_K

    def fill_pad(s, carry):
        slot_ref[s + MOE_BLOCK] = TOP_K * TOKENS + (s & (2 * MOE_BLOCK - 1))
        return carry

    @pl.when(i == 0)
    def _():
        lax.fori_loop(-MOE_BLOCK, 0, fill_pad, 0)
        for e in range(N_EXPERTS):
            lax.fori_loop((pend_ref[e - 1] if e else 0) + cnt_ref[e], pend_ref[e], fill_pad, 0)
        lax.fori_loop(nact_ref[0] * MOE_BLOCK, MOE_ROWS, fill_pad, 0)

    t0 = i * tc

    def scatter(r, carry):
        for kk in range(TOP_K):
            slot_ref[dest_ref[r * TOP_K + kk]] = kk * TOKENS + t0 + r
        return carry

    lax.fori_loop(0, tc, scatter, 0, unroll=8)


def _slot_table(dest_flat, counts, pend, nact):
    tc = 4096
    return pl.pallas_call(
        _slot_kernel,
        grid_spec=pltpu.PrefetchScalarGridSpec(
            num_scalar_prefetch=3,
            grid=(TOKENS // tc,),
            in_specs=[pl.BlockSpec((tc * TOP_K,), lambda i, *_: (i,), memory_space=pltpu.SMEM)],
            out_specs=pl.BlockSpec(memory_space=pltpu.SMEM),
        ),
        out_shape=jax.ShapeDtypeStruct((MOE_ROWS + MOE_BLOCK,), jnp.int32),
        compiler_params=_params("arbitrary"),
        name="moe_slots",
    )(pend, counts, nact, dest_flat)


def _moe_kernel(layer, be_ref, nact_ref, pend_ref, slot_ref,
                h_ref, wg_hbm, wl_hbm, wd_hbm, bg0, bl0, bd0, bg1, bl1, bd1, o4_ref,
                rbuf, wstage, wbf, gsem, ssem, wsem):
    step = pl.program_id(0)
    nact = nact_ref[0]
    rows = MOE_BLOCK * ROW_CHUNKS
    weights = (wg_hbm, wl_hbm, wd_hbm)
    zero = be_ref[0] >> 16
    xoff = (0, rows)
    yoff = (2 * rows, 3 * rows)
    aoff = 4 * rows + zero

    def region(off):
        return rbuf.at[pl.ds(pl.multiple_of(off, ROW_CHUNKS), rows)]

    def tile_at(buf, first):
        return buf.at[pl.ds(pl.multiple_of(first, ROW_CHUNKS), ROW_CHUNKS)]

    def gather_row(blk, r, q):
        tok = slot_ref[(blk + 1) * MOE_BLOCK + r] & (TOKENS - 1)
        return pltpu.make_async_copy(tile_at(h_ref, tok * ROW_CHUNKS),
                                     tile_at(rbuf, xoff[q] + r * ROW_CHUNKS), gsem.at[q])

    def scatter_row(blk, r, q):
        row = slot_ref[(blk + 1) * MOE_BLOCK + r]
        return pltpu.make_async_copy(tile_at(rbuf, yoff[q] + r * ROW_CHUNKS),
                                     tile_at(o4_ref, row * ROW_CHUNKS), ssem.at[q])

    def gather_wait(q):
        pltpu.make_async_copy(h_ref.at[pl.ds(0, rows)], region(xoff[q]), gsem.at[q]).wait()

    def scatter_wait(q):
        pltpu.make_async_copy(region(yoff[q]), o4_ref.at[pl.ds(0, rows)], ssem.at[q]).wait()

    def weights_start(e):
        for i, w in enumerate(weights):
            pltpu.make_async_copy(w.at[layer, e], wstage.at[i], wsem.at[i]).start(priority=1)

    def weights_wait():
        for i, w in enumerate(weights):
            pltpu.make_async_copy(w.at[layer, 0], wstage.at[i], wsem.at[i]).wait()

    @pl.when(step == 0)
    def _():
        for q in range(2):
            rbuf[pl.ds((2 + q) * rows, rows), :] = jnp.zeros((rows, LANES), U32)
            dump = o4_ref.at[pl.ds(TOP_K * TOKENS * ROW_CHUNKS + q * rows, rows)]
            cp = pltpu.make_async_copy(region(yoff[q]), dump, ssem.at[q])
            cp.start()
            cp.wait()

        def first_rows(r, carry):
            gather_row(0, r, 0).start()
            return carry

        lax.fori_loop(0, MOE_BLOCK, first_rows, 0, unroll=8)
        weights_start(be_ref[0])

    def run_block(b, q, bg_ref, bl_ref, bd_ref):
        o = 1 - q

        @pl.when(b < nact)
        def _():
            e = be_ref[b]

            @pl.when((b == 0) | (e != be_ref[jnp.maximum(b - 1, 0)]))
            def _():
                weights_wait()
                for i in range(3):
                    wbf[i] = wstage[i].astype(BF16)
                nxt = pend_ref[e] // MOE_BLOCK

                @pl.when(nxt < nact)
                def _():
                    weights_start(be_ref[jnp.minimum(nxt, MOE_NBLOCKS - 1)])

            gather_wait(q)

            @pl.when(b >= 1)
            def _():
                scatter_wait(q)

            x = _load_row_tiles(rbuf, xoff[q] + zero, MOE_BLOCK).astype(BF16)
            nb = jnp.minimum(b + 1, nact - 1)
            for r in range(MOE_BLOCK):
                gather_row(nb, r, o).start()
            for r in range(MOE_BLOCK):
                scatter_row(b - 1, r, o).start(priority=r % 2)

            g = jnp.minimum(_dot(x, wbf[0]) + bg_ref[...], SWIGLU_LIMIT)
            lin = jnp.clip(_dot(x, wbf[1]) + bl_ref[...], -SWIGLU_LIMIT, SWIGLU_LIMIT)
            a = g * jax.nn.sigmoid(SWIGLU_ALPHA * g) * (lin + 1.0)
            for ch in range(MODEL_CHUNKS):
                rbuf[pl.ds(aoff + ch * MOE_BLOCK, MOE_BLOCK), :] = lax.bitcast_convert_type(
                    a[:, ch * LANES:(ch + 1) * LANES], U32)
            a = jnp.concatenate(
                [lax.bitcast_convert_type(rbuf[pl.ds(aoff + ch * MOE_BLOCK, MOE_BLOCK), :], F32)
                 for ch in range(MODEL_CHUNKS)], axis=1)
            _store_row_tiles(rbuf, yoff[q] + zero, _dot(a.astype(BF16), wbf[2]) + bd_ref[...])

            @pl.when(b == nact - 1)
            def _():
                def last_rows(r, carry):
                    scatter_row(b, r, q).start()
                    return carry

                lax.fori_loop(0, MOE_BLOCK, last_rows, 0, unroll=8)
                gather_wait(o)
                scatter_wait(o)
                scatter_wait(q)

    run_block(2 * step, 0, bg0, bl0, bd0)
    run_block(2 * step + 1, 1, bg1, bl1, bd1)


def _moe_experts(l, h_tiles, block_e, nact, pend, slot_row, w_glu, b_glu, w_lin, b_lin, w_down, b_down):
    rows = MOE_BLOCK * ROW_CHUNKS

    def bspec(q):
        return pl.BlockSpec((None, None, 1, D_MODEL),
                            lambda s, be, na, pe, sl: (l, be[jnp.minimum(2 * s + q, na[0] - 1)], 0, 0))

    b3 = lambda b: b.reshape(DEPTH, N_EXPERTS, 1, D_MODEL)
    hbm = pl.BlockSpec(memory_space=pl.ANY)
    biases = (b3(b_glu), b3(b_lin), b3(b_down))
    return pl.pallas_call(
        functools.partial(_moe_kernel, l),
        grid_spec=pltpu.PrefetchScalarGridSpec(
            num_scalar_prefetch=4,
            grid=(MOE_NBLOCKS // 2,),
            in_specs=[hbm, hbm, hbm, hbm] + [bspec(0)] * 3 + [bspec(1)] * 3,
            out_specs=hbm,
            scratch_shapes=[
                pltpu.VMEM((4 * rows + MODEL_CHUNKS * MOE_BLOCK, LANES), U32),
                pltpu.VMEM((3, D_MODEL, D_MODEL), F32),
                pltpu.VMEM((3, D_MODEL, D_MODEL), BF16),
                pltpu.SemaphoreType.DMA((2,)), pltpu.SemaphoreType.DMA((2,)),
                pltpu.SemaphoreType.DMA((3,))],
        ),
        out_shape=jax.ShapeDtypeStruct(((TOP_K * TOKENS + 2 * MOE_BLOCK) * ROW_CHUNKS, LANES), U32),
        compiler_params=_params("arbitrary"),
        name="moe_experts",
    )(block_e, nact, pend, slot_row, h_tiles, w_glu, w_lin, w_down, *biases, *biases)


def _combine_kernel(y0_ref, y1_ref, y2_ref, y3_ref, gt_ref, x_ref, gf_ref, lng_ref, lnb_ref, xo_ref):
    tm = x_ref.shape[0]
    gt = gt_ref[...]
    o = gt[:, 0:1] * _load_row_tiles(y0_ref, 0, tm)
    for kk, y_ref in enumerate((y1_ref, y2_ref, y3_ref), start=1):
        o = o + gt[:, kk:kk + 1] * _load_row_tiles(y_ref, 0, tm)
    xo_ref[...] = (_ln(DEEPNORM_ALPHA * x_ref[...] + gf_ref[...] * o, LN_EPS) * lng_ref[...]
                   + lnb_ref[...])


def _combine(o4, gates, x, modl, ln_g, ln_b):
    tm = ROW_TILE
    row = pl.BlockSpec((tm, D_MODEL), lambda i: (i, 0))
    vec = pl.BlockSpec((1, D_MODEL), lambda i: (0, 0))
    nt = TOKENS // tm
    yspec = lambda kk: pl.BlockSpec((tm * ROW_CHUNKS, LANES), lambda i: (kk * nt + i, 0))
    return pl.pallas_call(
        _combine_kernel,
        grid=(nt,),
        in_specs=[yspec(0), yspec(1), yspec(2), yspec(3),
                  pl.BlockSpec((tm, 8), lambda i: (i, 0)),
                  row, _mod_spec(5, tm), vec, vec],
        out_specs=row,
        out_shape=jax.ShapeDtypeStruct((TOKENS, D_MODEL), F32),
        compiler_params=_params("arbitrary"),
        name="moe_combine",
    )(o4, o4, o4, o4, gates, x, modl, ln_g.reshape(1, -1), ln_b.reshape(1, -1))


def _rope_cos_sin(positions):
    half = QK_ROPE // 2
    per_row = LANES // QK_ROPE
    inv_freq = ROPE_THETA ** (-jnp.arange(0, QK_ROPE, 2, dtype=F32) / QK_ROPE)
    freq = jnp.tile(jnp.concatenate([inv_freq, inv_freq]), per_row)
    is_cos = jnp.tile(jnp.arange(QK_ROPE) < half, per_row)
    pos = positions.reshape(TOKENS // per_row, per_row, 1).astype(F32)
    ang = jnp.broadcast_to(pos, (TOKENS // per_row, per_row, QK_ROPE)).reshape(-1, LANES) * freq
    return jnp.where(is_cos, jnp.cos(ang), jnp.sin(ang)).reshape(TOKENS, QK_ROPE)


def _moe_layer(l, x, h, top_e, gates, modl, ln_g, ln_b, w_glu, b_glu, w_lin, b_lin, w_down, b_down):
    dest_flat, block_e, counts, pend, nact = _rank(top_e)
    slot_row = _slot_table(dest_flat, counts, pend, nact)
    o4 = _moe_experts(l, h, block_e, nact, pend, slot_row, w_glu, b_glu, w_lin, b_lin, w_down, b_down)
    return _combine(o4, gates, x, modl, ln_g[l], ln_b[l])


def kernel(x, c, positions, ada_w, ada_b, ln_mix_g, ln_mix_b, ln_ffn_g, ln_ffn_b, mla_w_in, mla_q_norm_g, mla_kv_norm_g, mla_w_uq, mla_w_uk, mla_w_uv, mix_a_w_out, conv_w_in, conv_w, conv_w_out, moe_w_router, moe_b_router, moe_w_glu, moe_b_glu, moe_w_lin, moe_b_lin, moe_w_down, moe_b_down):
    mod = _modulation(c, ada_w, ada_b)
    rope_cs = _rope_cos_sin(positions)
    x = x.reshape(TOKENS, D_MODEL)
    for l in range(DEPTH):
        modl = mod[l]
        i = l // 2
        if l % 2 == 0:
            q, k, v, a_t, b_t = _pre0(x, modl, rope_cs, mla_w_in[i], mla_q_norm_g[i],
                                      mla_kv_norm_g[i], mla_w_uq[i], mla_w_uk[i], mla_w_uv[i])
            o_a = _attention(q, k, v)
            o_b = _fnet(a_t, b_t)
            x, h, top_e, gates = _post0(o_a, o_b, mix_a_w_out[i], x, modl, ln_mix_g[l], ln_mix_b[l],
                                        moe_w_router[l], moe_b_router[l])
        else:
            gate_b, u = _convin(x, modl, conv_w_in[i])
            x, h, top_e, gates = _post1(u, gate_b, conv_w[i], conv_w_out[i], x, modl, ln_mix_g[l],
                                        ln_mix_b[l], moe_w_router[l], moe_b_router[l])
        x = _moe_layer(l, x, h, top_e, gates, modl, ln_ffn_g, ln_ffn_b, moe_w_glu, moe_b_glu,
                       moe_w_lin, moe_b_lin, moe_w_down, moe_b_down)
    return x.reshape(BATCH, SEQ, D_MODEL)
```
